```python
import functools
import jax, jax.numpy as jnp
from jax import lax
import numpy as np

D_MODEL = 1024
BATCH = 2
SEQ = 8192
DEPTH = 2
DEC_BATCH = 32
DEC_SEQ = 4
PAST_LEN = 8192
PAGE_SIZE = 128

SB_HEAD_DIM = 64
SB_WIDTH = D_MODEL // 2
SB_HEADS = SB_WIDTH // SB_HEAD_DIM
SB_BLOCK = 128
SB_BIAS_INIT = -8.0
SC_WIDTH = D_MODEL // 2
SC_TAPS = 3
GDN_HEAD_DIM = 128
GDN_WIDTH = D_MODEL // 2
GDN_HEADS = GDN_WIDTH // GDN_HEAD_DIM
GDN_TAPS = 4
GDN_CHUNK = 64
D_FF = 4 * D_MODEL
N_BRANCH = 3
N_IN = 3 * SB_WIDTH + 3 * SC_WIDTH + 4 * GDN_WIDTH + 2 * GDN_HEADS + N_BRANCH * D_MODEL
EPS = 1e-6

kernel_name = "hybrid_sb_shortconv_gdn_decoder_step"


def rmsnorm(x, g):
    xf = x.astype(jnp.float32)
    y = xf * lax.rsqrt(jnp.mean(xf * xf, axis=-1, keepdims=True) + EPS)
    return (y * g.astype(jnp.float32)).astype(x.dtype)


def l2norm(x):
    xf = x.astype(jnp.float32)
    return xf * lax.rsqrt(jnp.sum(xf * xf, axis=-1, keepdims=True) + EPS)


def split_cols(x, sizes):
    out, start = [], 0
    for s in sizes:
        out.append(x[..., start:start + s])
        start += s
    return out


def causal_conv(x, w, buf):
    K, T = w.shape[0], x.shape[1]
    xp = jnp.concatenate([buf.astype(x.dtype), x], axis=1)
    y = w[0] * xp[:, 0:T]
    for j in range(1, K):
        y = y + w[j] * xp[:, j:j + T]
    return y, xp[:, xp.shape[1] - (K - 1):]


def sb_attend(q, k, v, bias, q_pos, k_pos):
    z = jnp.einsum('bqhd,bkhd->bhqk', q.astype(jnp.float32), k.astype(jnp.float32)) * (SB_HEAD_DIM ** -0.5)
    z = z + bias.astype(jnp.float32)[None, :, None, None]
    causal = k_pos[None, :] < q_pos[:, None]
    log_1mb = jnp.where(causal, jax.nn.log_sigmoid(-z), 0.0)
    rest = lax.cumsum(log_1mb, axis=3, reverse=True) - log_1mb
    a = jnp.where(causal, jnp.exp(jax.nn.log_sigmoid(z) + rest), 0.0)
    return jnp.einsum('bhqk,bkhd->bqhd', a.astype(v.dtype), v)


def sb_prompt(q, k, v, bias):
    B, T, H, d = q.shape
    nb = T // SB_BLOCK
    qb = q.reshape(B, nb, SB_BLOCK, H, d).transpose(1, 0, 2, 3, 4)
    k_pos = jnp.arange(T)

    def block(args):
        qi, i = args
        return sb_attend(qi, k, v, bias, i * SB_BLOCK + jnp.arange(SB_BLOCK), k_pos)

    out = lax.map(block, (qb, jnp.arange(nb)))
    return out.transpose(1, 0, 2, 3, 4).reshape(B, T, H, d)


def sb_decode(q, k, v, bias, k_past, v_past):
    past, T = k_past.shape[1], q.shape[1]
    k_all = jnp.concatenate([k_past, k.astype(k_past.dtype)], axis=1)
    v_all = jnp.concatenate([v_past, v.astype(v_past.dtype)], axis=1)
    return sb_attend(q, k_all, v_all, bias, past + jnp.arange(T), jnp.arange(past + T))


def gated_delta_rule(q, k, v, g, beta, s0):
    B, T, H, _ = q.shape
    C = min(GDN_CHUNK, T)
    pad = (-T) % C
    n = (T + pad) // C

    def prep(a):
        a = a.astype(jnp.float32)
        a = jnp.pad(a, [(0, 0), (0, pad)] + [(0, 0)] * (a.ndim - 2))
        a = a.reshape((B, n, C) + a.shape[2:])
        return a.transpose((1, 0, 3, 2) + tuple(range(4, a.ndim)))

    incl = jnp.tril(jnp.ones((C, C), dtype=bool))
    strict = jnp.tril(jnp.ones((C, C), dtype=bool), -1)
    eye = jnp.eye(C, dtype=jnp.float32)

    def step(S, inp):
        qc, kc, vc, gc, bc = inp
        G = jnp.cumsum(gc, axis=-1)
        decay = jnp.where(incl, jnp.exp(jnp.where(incl, G[..., :, None] - G[..., None, :], 0.0)), 0.0)
        A = jnp.where(strict, bc[..., :, None] * decay * jnp.einsum('bhid,bhjd->bhij', kc, kc), 0.0)
        rhs = bc[..., None] * (vc - jnp.exp(G)[..., None] * jnp.einsum('bhid,bhdv->bhiv', kc, S))
        W = lax.linalg.triangular_solve(eye + A, rhs, left_side=True, lower=True)
        o = jnp.exp(G)[..., None] * jnp.einsum('bhtd,bhdv->bhtv', qc, S) + jnp.einsum(
            'bhti,bhiv->bhtv', jnp.einsum('bhtd,bhid->bhti', qc, kc) * decay, W)
        G_last = G[..., -1:]
        S_new = jnp.exp(G_last)[..., None] * S + jnp.einsum(
            'bhid,bhiv->bhdv', kc * jnp.exp(G_last - G)[..., None], W)
        return S_new, o

    S, o = lax.scan(step, s0.astype(jnp.float32), (prep(q), prep(k), prep(v), prep(g), prep(beta)))
    o = o.transpose(1, 0, 3, 2, 4).reshape(B, n * C, H, -1)[:, :T]
    return o, S.astype(s0.dtype)


def trunk_layer(x, attend, buf_b, buf_c, s0, norm1_g, w_in, qn_g, kn_g, sb_bias, w_conv_b, w_conv_c,
                a_log, dt_bias, onorm_g, w_pa, w_pb, w_pc, w_o, norm2_g, w_up, w_down):
    B, T, _ = x.shape
    h = rmsnorm(x, norm1_g)
    proj = h @ w_in
    (qa, ka, va, b_gate, c_gate, x_in, qkv_c, z_c, beta_logit, a_logit, gate_logit) = split_cols(
        proj, [SB_WIDTH] * 3 + [SC_WIDTH] * 3 + [3 * GDN_WIDTH, GDN_WIDTH, GDN_HEADS, GDN_HEADS, N_BRANCH * D_MODEL])
    qa = rmsnorm(qa.reshape(B, T, SB_HEADS, SB_HEAD_DIM), qn_g)
    ka = rmsnorm(ka.reshape(B, T, SB_HEADS, SB_HEAD_DIM), kn_g)
    va = va.reshape(B, T, SB_HEADS, SB_HEAD_DIM)
    out_a = attend(qa, ka, va, sb_bias).reshape(B, T, SB_WIDTH)
    conv_b, new_buf_b = causal_conv(c_gate * x_in, w_conv_b, buf_b)
    out_b = b_gate * conv_b
    conv_c, new_buf_c = causal_conv(qkv_c, w_conv_c, buf_c)
    qc, kc, vc = split_cols(jax.nn.silu(conv_c), [GDN_WIDTH] * 3)
    qc = l2norm(qc.reshape(B, T, GDN_HEADS, GDN_HEAD_DIM)) * (GDN_HEAD_DIM ** -0.5)
    kc = l2norm(kc.reshape(B, T, GDN_HEADS, GDN_HEAD_DIM))
    vc = vc.reshape(B, T, GDN_HEADS, GDN_HEAD_DIM)
    beta = jax.nn.sigmoid(beta_logit.astype(jnp.float32))
    g = -jnp.exp(a_log.astype(jnp.float32)) * jax.nn.softplus(a_logit.astype(jnp.float32) + dt_bias.astype(jnp.float32))
    o_c, s_new = gated_delta_rule(qc, kc, vc, g, beta, s0)
    o_c = (rmsnorm(o_c.astype(x.dtype), onorm_g) * jax.nn.silu(z_c.reshape(B, T, GDN_HEADS, GDN_HEAD_DIM))).reshape(B, T, GDN_WIDTH)
    gates = jax.nn.sigmoid(gate_logit.astype(jnp.float32)).reshape(B, T, N_BRANCH, D_MODEL).astype(x.dtype)
    mix = gates[:, :, 0] * (out_a @ w_pa) + gates[:, :, 1] * (out_b @ w_pb) + gates[:, :, 2] * (o_c @ w_pc)
    x = x + mix @ w_o
    x = x + jnp.square(jax.nn.relu(rmsnorm(x, norm2_g) @ w_up)) @ w_down
    return x, ka, va, new_buf_b, new_buf_c, s_new


def setup_inputs(seed: int = 0) -> dict:
    key = jax.random.key(seed)
    ks = jax.random.split(key, 25)
    n_pages = PAST_LEN // PAGE_SIZE
    n_used = DEC_BATCH * n_pages
    n_phys = n_used + n_used // 4
    nrm = jax.random.normal

    def gain(k, shape):
        return 1.0 + 0.01 * nrm(k, shape, jnp.float32)

    dt = jnp.exp(jax.random.uniform(ks[12], (DEPTH, GDN_HEADS), jnp.float32, np.log(0.001), np.log(0.1)))
    return {
        'x_prompt': nrm(ks[0], (BATCH, SEQ, D_MODEL), jnp.float32),
        'x_sample': nrm(ks[1], (DEC_BATCH, DEC_SEQ, D_MODEL), jnp.float32),
        'cache_k': nrm(ks[2], (DEPTH, n_phys, PAGE_SIZE, SB_HEADS, SB_HEAD_DIM), jnp.float32),
        'cache_v': nrm(ks[3], (DEPTH, n_phys, PAGE_SIZE, SB_HEADS, SB_HEAD_DIM), jnp.float32),
        'page_table': jax.random.permutation(ks[4], n_phys)[:n_used].reshape(DEC_BATCH, n_pages).astype(jnp.int32),
        'state_conv_b': nrm(ks[5], (DEPTH, DEC_BATCH, SC_TAPS - 1, SC_WIDTH), jnp.float32),
        'state_conv_c': nrm(ks[6], (DEPTH, DEC_BATCH, GDN_TAPS - 1, 3 * GDN_WIDTH), jnp.float32),
        'state_gdn': 0.1 * nrm(ks[7], (DEPTH, DEC_BATCH, GDN_HEADS, GDN_HEAD_DIM, GDN_HEAD_DIM), jnp.float32),
        'norm1_g': gain(ks[8], (DEPTH, D_MODEL)),
        'w_in': nrm(ks[9], (DEPTH, D_MODEL, N_IN), jnp.float32) * D_MODEL ** -0.5,
        'qn_g': gain(ks[10], (DEPTH, SB_HEAD_DIM)),
        'kn_g': gain(ks[11], (DEPTH, SB_HEAD_DIM)),
        'sb_bias': SB_BIAS_INIT + 0.1 * nrm(ks[24], (DEPTH, SB_HEADS), jnp.float32),
        'w_conv_b': nrm(ks[13], (DEPTH, SC_TAPS, SC_WIDTH), jnp.float32) * SC_TAPS ** -0.5,
        'w_conv_c': nrm(ks[14], (DEPTH, GDN_TAPS, 3 * GDN_WIDTH), jnp.float32) * GDN_TAPS ** -0.5,
        'a_log': jnp.log(jax.random.uniform(ks[15], (DEPTH, GDN_HEADS), jnp.float32, 1.0, 16.0)),
        'dt_bias': dt + jnp.log(-jnp.expm1(-dt)),
        'onorm_g': gain(ks[16], (DEPTH, GDN_HEAD_DIM)),
        'w_pa': nrm(ks[17], (DEPTH, SB_WIDTH, D_MODEL), jnp.float32) * SB_WIDTH ** -0.5,
        'w_pb': nrm(ks[18], (DEPTH, SC_WIDTH, D_MODEL), jnp.float32) * SC_WIDTH ** -0.5,
        'w_pc': nrm(ks[19], (DEPTH, GDN_WIDTH, D_MODEL), jnp.float32) * GDN_WIDTH ** -0.5,
        'w_o': nrm(ks[20], (DEPTH, D_MODEL, D_MODEL), jnp.float32) * D_MODEL ** -0.5,
        'norm2_g': gain(ks[21], (DEPTH, D_MODEL)),
        'w_up': nrm(ks[22], (DEPTH, D_MODEL, D_FF), jnp.float32) * D_MODEL ** -0.5,
        'w_down': nrm(ks[23], (DEPTH, D_FF, D_MODEL), jnp.float32) * D_FF ** -0.5,
    }


def reference(x_prompt, x_sample, cache_k, cache_v, page_table, state_conv_b, state_conv_c, state_gdn,
              norm1_g, w_in, qn_g, kn_g, sb_bias, w_conv_b, w_conv_c, a_log, dt_bias, onorm_g,
              w_pa, w_pb, w_pc, w_o, norm2_g, w_up, w_down):
    dec_b, n_pages = page_table.shape
    xp, xs = x_prompt, x_sample
    pk, pv, pcb, pcc, ps = [], [], [], [], []
    sk, sv, scb, scc, ss = [], [], [], [], []
    for l in range(DEPTH):
        w = (norm1_g[l], w_in[l], qn_g[l], kn_g[l], sb_bias[l], w_conv_b[l], w_conv_c[l], a_log[l], dt_bias[l],
             onorm_g[l], w_pa[l], w_pb[l], w_pc[l], w_o[l], norm2_g[l], w_up[l], w_down[l])
        buf_b0 = jnp.zeros((xp.shape[0], SC_TAPS - 1, SC_WIDTH), xp.dtype)
        buf_c0 = jnp.zeros((xp.shape[0], GDN_TAPS - 1, 3 * GDN_WIDTH), xp.dtype)
        s_zero = jnp.zeros((xp.shape[0], GDN_HEADS, GDN_HEAD_DIM, GDN_HEAD_DIM), state_gdn.dtype)
        xp, k_p, v_p, cb_p, cc_p, s_p = trunk_layer(xp, sb_prompt, buf_b0, buf_c0, s_zero, *w)
        k_past = cache_k[l][page_table].reshape(dec_b, n_pages * PAGE_SIZE, SB_HEADS, SB_HEAD_DIM)
        v_past = cache_v[l][page_table].reshape(dec_b, n_pages * PAGE_SIZE, SB_HEADS, SB_HEAD_DIM)
        attend = functools.partial(sb_decode, k_past=k_past, v_past=v_past)
        xs, k_s, v_s, cb_s, cc_s, s_s = trunk_layer(xs, attend, state_conv_b[l], state_conv_c[l], state_gdn[l], *w)
        pk.append(k_p); pv.append(v_p); pcb.append(cb_p); pcc.append(cc_p); ps.append(s_p)
        sk.append(k_s); sv.append(v_s); scb.append(cb_s); scc.append(cc_s); ss.append(s_s)
    return (xp, xs,
            jnp.stack(pk), jnp.stack(pv), jnp.stack(pcb), jnp.stack(pcc), jnp.stack(ps),
            jnp.stack(sk), jnp.stack(sv), jnp.stack(scb), jnp.stack(scc), jnp.stack(ss))
```

```python
import functools
import math

import jax
import jax.numpy as jnp
from jax import lax
from jax.experimental import pallas as pl
from jax.experimental.pallas import tpu as pltpu

F32 = jnp.float32
BF16 = jnp.bfloat16
HIGHEST = lax.Precision.HIGHEST

EPS = 1e-6
LOG2E = 1.4426950408889634

SB_HEADS = 8
SB_HEAD_DIM = 64
SB_WIDTH = SB_HEADS * SB_HEAD_DIM
SC_WIDTH = 512
SC_TAPS = 3
GDN_HEADS = 4
GDN_HEAD_DIM = 128
GDN_WIDTH = GDN_HEADS * GDN_HEAD_DIM
GDN_TAPS = 4
GDN_CHUNK = 64
PAGE_SIZE = 128
N_BRANCH = 3
D_MODEL = 1024

REST_GATES = 0
REST_BCX = REST_GATES + N_BRANCH * D_MODEL
REST_QKV = REST_BCX + 3 * SC_WIDTH
REST_Z = REST_QKV + 3 * GDN_WIDTH

LANES = 128
SUBLANES = 8
COL_TILE = 512
VMEM_LIMIT = 48 * 1024 * 1024

SB_TQ = 256
SB_TK = 256
DEC_TQ = 8
DEC_PAGES_PER_STEP = 8


def _cparams(sem):
    return pltpu.CompilerParams(dimension_semantics=sem, vmem_limit_bytes=VMEM_LIMIT)


def _dot(a, b, precision=None):
    return jnp.dot(a, b, preferred_element_type=F32, precision=precision)


def _dot_nt(a, b, precision=None):
    return lax.dot_general(a, b, (((1,), (1,)), ((), ())), preferred_element_type=F32, precision=precision)


def _dot_tn(a, b, precision=None):
    return lax.dot_general(a, b, (((0,), (0,)), ((), ())), preferred_element_type=F32, precision=precision)


def _softplus2(z):
    return jnp.maximum(z, 0.0) + jnp.log(1.0 + jnp.exp2(-jnp.abs(z))) * LOG2E


def _proj_kernel(x_ref, g_ref, w_ref, ws_ref, qg_ref, kg_ref, p_ref,
                 q_ref, k_ref, v_ref, rest_ref, small_ref, h_ref, *, q_scale):
    j = pl.program_id(1)

    @pl.when(j == 0)
    def _():
        x = x_ref[...]
        ms = jnp.mean(x * x, axis=-1, keepdims=True)
        h = (x * lax.rsqrt(ms + EPS) * g_ref[...]).astype(BF16)
        h_ref[...] = h
        small_ref[...] = _dot(h, ws_ref[...])

    y = _dot(h_ref[...], w_ref[...])

    def head_norm(y, gain):
        sq = y * y
        hi = sq.astype(BF16)
        lo = (sq - hi.astype(F32)).astype(BF16)
        ms = _dot(hi, p_ref[...]) + _dot(lo, p_ref[...])
        return y * lax.rsqrt(ms + EPS) * gain

    @pl.when(j == 0)
    def _():
        q_ref[...] = (head_norm(y, qg_ref[...]) * q_scale).astype(BF16)

    @pl.when(j == 1)
    def _():
        k_ref[...] = head_norm(y, kg_ref[...])

    @pl.when(j == 2)
    def _():
        v_ref[...] = y

    @pl.when(j >= 3)
    def _():
        rest_ref[...] = y


def _proj(x2d, g, w_main, w_small, qg, kg, pmat, tm):
    n, d = x2d.shape
    ncol = w_main.shape[1] // COL_TILE
    nrest = w_main.shape[1] - 3 * COL_TILE
    kern = functools.partial(_proj_kernel, q_scale=SB_HEAD_DIM ** -0.5 * LOG2E)
    row = lambda i, j: (i, 0)
    const = lambda i, j: (0, 0)
    return pl.pallas_call(
        kern,
        grid=(n // tm, ncol),
        in_specs=[
            pl.BlockSpec((tm, d), row),
            pl.BlockSpec((1, d), const),
            pl.BlockSpec((d, COL_TILE), lambda i, j: (0, j)),
            pl.BlockSpec((d, LANES), const),
            pl.BlockSpec((1, COL_TILE), const),
            pl.BlockSpec((1, COL_TILE), const),
            pl.BlockSpec((COL_TILE, COL_TILE), const),
        ],
        out_specs=[
            pl.BlockSpec((tm, COL_TILE), row),
            pl.BlockSpec((tm, COL_TILE), row),
            pl.BlockSpec((tm, COL_TILE), row),
            pl.BlockSpec((tm, COL_TILE), lambda i, j: (i, jnp.maximum(j - 3, 0))),
            pl.BlockSpec((tm, LANES), row),
        ],
        out_shape=[
            jax.ShapeDtypeStruct((n, COL_TILE), BF16),
            jax.ShapeDtypeStruct((n, COL_TILE), F32),
            jax.ShapeDtypeStruct((n, COL_TILE), F32),
            jax.ShapeDtypeStruct((n, nrest), F32),
            jax.ShapeDtypeStruct((n, LANES), F32),
        ],
        scratch_shapes=[pltpu.VMEM((tm, d), BF16)],
        compiler_params=_cparams(("parallel", "arbitrary")),
        name="proj",
    )(x2d, g, w_main, w_small, qg, kg, pmat)


def _sb_prompt_kernel(q_ref, k_ref, v_ref, u_ref, o_ref, acc_ref, r_ref):
    qi = pl.program_id(1)
    q = q_ref[0]
    tq = q.shape[0]
    tk = u_ref.shape[0]

    def tile(j, masked):
        start = pl.multiple_of(j * tk, tk)
        kt = k_ref[0, pl.ds(start, tk), :]
        vt = v_ref[0, pl.ds(start, tk), :]
        z = _dot_nt(q, kt)
        sp = _softplus2(z)
        if masked:
            valid = (lax.broadcasted_iota(jnp.int32, (tq, tk), 1)
                     < lax.broadcasted_iota(jnp.int32, (tq, tk), 0))
            sp = jnp.where(valid, sp, 0.0)
        m = _dot(sp.astype(BF16), u_ref[...])
        a = jnp.exp2(z - m)
        if masked:
            a = jnp.where(valid, a, 0.0)
        pv = _dot(a.astype(BF16), vt)
        return pv, m[:, 0:1]

    pv, m0 = tile(qi, True)
    acc_ref[...] = pv
    r_ref[...] = -m0

    def body(s, carry):
        pv, m0 = tile(qi - 1 - s, False)
        r = r_ref[...]
        acc_ref[...] += jnp.exp2(r) * pv
        r_ref[...] = r - m0
        return carry

    lax.fori_loop(0, qi, body, 0)
    o_ref[0] = acc_ref[...].astype(o_ref.dtype)


def _sb_prompt(q_aug, k_aug, v_h, umat):
    bh, t, _ = q_aug.shape
    return pl.pallas_call(
        _sb_prompt_kernel,
        grid=(bh, t // SB_TQ),
        in_specs=[
            pl.BlockSpec((1, SB_TQ, LANES), lambda b, i: (b, i, 0)),
            pl.BlockSpec((1, t, LANES), lambda b, i: (b, 0, 0)),
            pl.BlockSpec((1, t, SB_HEAD_DIM), lambda b, i: (b, 0, 0)),
            pl.BlockSpec((SB_TK, SB_TK), lambda b, i: (0, 0)),
        ],
        out_specs=pl.BlockSpec((1, SB_TQ, SB_HEAD_DIM), lambda b, i: (b, i, 0)),
        out_shape=jax.ShapeDtypeStruct((bh, t, SB_HEAD_DIM), BF16),
        scratch_shapes=[pltpu.VMEM((SB_TQ, SB_HEAD_DIM), F32), pltpu.VMEM((SB_TQ, 1), F32)],
        compiler_params=_cparams(("parallel", "arbitrary")),
        name="sb_prompt",
    )(q_aug, k_aug, v_h, umat)


def _sb_decode_kernel(pt_ref, qbd_ref, kn_ref, vn_ref, bias_ref, sel_ref, lmat_ref, *rest):
    del pt_ref
    npg = DEC_PAGES_PER_STEP
    kp_refs = rest[:npg]
    vp_refs = rest[npg:2 * npg]
    o_ref, acc_ref, r_ref = rest[2 * npg:]
    g = pl.program_id(1)

    def block(kb, vb, valid):
        z = _dot(kb.astype(BF16), qbd_ref[0]) + bias_ref[...]
        sp = _softplus2(z)
        if valid is not None:
            sp = jnp.where(valid, sp, 0.0)
        m = _dot(lmat_ref[...], sp.astype(BF16))
        a = jnp.exp2(z - m + r_ref[...])
        if valid is not None:
            a = jnp.where(valid, a, 0.0)
        acc_ref[...] += _dot_tn(a.astype(BF16), vb.astype(BF16))
        r_ref[...] -= m[0:1, :]

    @pl.when(g == 0)
    def _():
        acc_ref[...] = jnp.zeros_like(acc_ref)
        r_ref[...] = jnp.zeros_like(r_ref)
        key = lax.broadcasted_iota(jnp.int32, (PAGE_SIZE, LANES), 0)
        qry = lax.broadcasted_iota(jnp.int32, (PAGE_SIZE, LANES), 1) % DEC_TQ
        block(kn_ref[0], vn_ref[0], key < qry)

    for i in reversed(range(npg)):
        block(kp_refs[i][0], vp_refs[i][0], None)

    @pl.when(g == pl.num_programs(1) - 1)
    def _():
        acc = acc_ref[...]
        rhead = lax.broadcasted_iota(jnp.int32, acc.shape, 0) // DEC_TQ
        chead = lax.broadcasted_iota(jnp.int32, acc.shape, 1) // SB_HEAD_DIM
        picked = jnp.where(rhead == chead, acc, 0.0)
        o_ref[0] = _dot(sel_ref[...], picked, HIGHEST).astype(o_ref.dtype)


def _sb_decode(page_ids, qbd, k_new, v_new, bias_row, sel, lmat, cache_k2, cache_v2):
    nb, n_pages = page_ids.shape
    npg = DEC_PAGES_PER_STEP
    ngroups = n_pages // npg

    def page_spec(i):
        return pl.BlockSpec((1, PAGE_SIZE, SB_WIDTH),
                            lambda b, g, pt: (pt[b, (ngroups - 1 - g) * npg + i], 0, 0))

    per_b = lambda b, g, pt: (b, 0, 0)
    const = lambda b, g, pt: (0, 0)
    grid_spec = pltpu.PrefetchScalarGridSpec(
        num_scalar_prefetch=1,
        grid=(nb, ngroups),
        in_specs=[
            pl.BlockSpec((1, SB_WIDTH, LANES), per_b),
            pl.BlockSpec((1, PAGE_SIZE, SB_WIDTH), per_b),
            pl.BlockSpec((1, PAGE_SIZE, SB_WIDTH), per_b),
            pl.BlockSpec((1, LANES), const),
            pl.BlockSpec((DEC_TQ, LANES), const),
            pl.BlockSpec((PAGE_SIZE, PAGE_SIZE), const),
        ] + [page_spec(i) for i in range(npg)] * 2,
        out_specs=pl.BlockSpec((1, DEC_TQ, SB_WIDTH), per_b),
        scratch_shapes=[
            pltpu.VMEM((LANES, SB_WIDTH), F32),
            pltpu.VMEM((1, LANES), F32),
        ],
    )
    return pl.pallas_call(
        _sb_decode_kernel,
        grid_spec=grid_spec,
        out_shape=jax.ShapeDtypeStruct((nb, DEC_TQ, SB_WIDTH), BF16),
        compiler_params=_cparams(("parallel", "arbitrary")),
        name="sb_decode",
    )(page_ids, qbd, k_new, v_new, bias_row, sel, lmat, *([cache_k2] * npg), *([cache_v2] * npg))


def _convb_kernel(b_ref, c_ref, x_ref, cp_ref, xp_ref, buf_ref, w_ref, o_ref, tail_ref, u_ref):
    i = pl.program_id(1)
    tt = b_ref.shape[1]
    u = c_ref[0] * x_ref[0]
    prev = jnp.where(i == 0, buf_ref[0], cp_ref[0] * xp_ref[0])
    u_ref[0:SUBLANES, :] = prev
    u_ref[SUBLANES:, :] = u
    base = SUBLANES - (SC_TAPS - 1)
    y = w_ref[0:1, :] * u_ref[base:base + tt, :]
    for j in range(1, SC_TAPS):
        y = y + w_ref[j:j + 1, :] * u_ref[base + j:base + j + tt, :]
    o_ref[0] = (b_ref[0] * y).astype(o_ref.dtype)
    tail_ref[0] = u[tt - SUBLANES:, :]


def _convb(rest3, buf8, w8, tt):
    nb, t, _ = rest3.shape
    r8 = tt // SUBLANES
    c0 = REST_BCX // COL_TILE
    cur = lambda c: pl.BlockSpec((1, tt, COL_TILE), lambda b, i: (b, i, c))
    prev = lambda c: pl.BlockSpec((1, SUBLANES, COL_TILE), lambda b, i: (b, jnp.maximum(i * r8 - 1, 0), c))
    return pl.pallas_call(
        _convb_kernel,
        grid=(nb, t // tt),
        in_specs=[cur(c0), cur(c0 + 1), cur(c0 + 2), prev(c0 + 1), prev(c0 + 2),
                  pl.BlockSpec((1, SUBLANES, COL_TILE), lambda b, i: (b, 0, 0)),
                  pl.BlockSpec((SUBLANES, COL_TILE), lambda b, i: (0, 0))],
        out_specs=[pl.BlockSpec((1, tt, COL_TILE), lambda b, i: (b, i, 0)),
                   pl.BlockSpec((1, SUBLANES, COL_TILE), lambda b, i: (b, 0, 0))],
        out_shape=[jax.ShapeDtypeStruct((nb, t, COL_TILE), BF16),
                   jax.ShapeDtypeStruct((nb, SUBLANES, COL_TILE), F32)],
        scratch_shapes=[pltpu.VMEM((tt + SUBLANES, COL_TILE), F32)],
        compiler_params=_cparams(("parallel", "arbitrary")),
        name="convb",
    )(rest3, rest3, rest3, rest3, rest3, buf8, w8)


def _gdn_kernel(x_ref, z_ref, sm_ref, xp_ref, buf_ref, s0_ref, w_ref, nega_ref, dtb_ref, og_ref,
                o_ref, s_out_ref, s_ref, xb_ref, *, n_valid):
    i = pl.program_id(1)
    c = x_ref.shape[1]
    hd = GDN_HEAD_DIM

    @pl.when(i == 0)
    def _():
        s_ref[...] = s0_ref[0]

    xb_ref[0:SUBLANES, :] = jnp.where(i == 0, buf_ref[0], xp_ref[0])
    xb_ref[SUBLANES:, :] = x_ref[0]
    base = SUBLANES - (GDN_TAPS - 1)
    y = w_ref[0:1, :] * xb_ref[base:base + c, :]
    for j in range(1, GDN_TAPS):
        y = y + w_ref[j:j + 1, :] * xb_ref[base + j:base + j + c, :]
    act = y * jax.nn.sigmoid(y)

    sm = sm_ref[0]
    beta_all = jax.nn.sigmoid(sm)
    pre = sm + dtb_ref[...]
    g_all = nega_ref[...] * (jnp.maximum(pre, 0.0) + jnp.log1p(jnp.exp(-jnp.abs(pre))))
    if n_valid < c:
        live = lax.broadcasted_iota(jnp.int32, (c, LANES), 0) < n_valid
        beta_all = jnp.where(live, beta_all, 0.0)
        g_all = jnp.where(live, g_all, 0.0)

    row = lax.broadcasted_iota(jnp.int32, (c, c), 0)
    col = lax.broadcasted_iota(jnp.int32, (c, c), 1)
    incl = col <= row
    strict = col < row
    eye = (row == col).astype(F32)
    gcum = _dot(incl.astype(F32), g_all, HIGHEST)
    lane = lax.broadcasted_iota(jnp.int32, (c, LANES), 1)

    for h in range(GDN_HEADS):
        qh = act[:, h * hd:(h + 1) * hd]
        kh = act[:, GDN_WIDTH + h * hd:GDN_WIDTH + (h + 1) * hd]
        vh = act[:, 2 * GDN_WIDTH + h * hd:2 * GDN_WIDTH + (h + 1) * hd]
        qh = qh * lax.rsqrt(jnp.sum(qh * qh, axis=-1, keepdims=True) + EPS) * (hd ** -0.5)
        kh = kh * lax.rsqrt(jnp.sum(kh * kh, axis=-1, keepdims=True) + EPS)
        b = beta_all[:, h:h + 1]
        gc = gcum[:, GDN_HEADS + h:GDN_HEADS + h + 1]
        xl = jnp.where(lane == 0, gc, jnp.where(lane == 1, 1.0, 0.0))
        yl = jnp.where(lane == 0, 1.0, jnp.where(lane == 1, -gc, 0.0))
        d = _dot_nt(xl, yl, HIGHEST)
        decay = jnp.where(incl, jnp.exp(jnp.where(incl, d, 0.0)), 0.0)
        eg = jnp.exp(gc)
        a = jnp.where(strict, b * decay * _dot_nt(kh, kh, HIGHEST), 0.0)
        tinv = eye - a
        p = a
        for _ in range(int(math.log2(c)) - 1):
            p = _dot(p, p, HIGHEST)
            tinv = tinv + _dot(tinv, p, HIGHEST)
        u = _dot(tinv, b * vh, HIGHEST)
        wk = _dot(tinv, (b * eg) * kh, HIGHEST)
        s = s_ref[h]
        w = u - _dot(wk, s, HIGHEST)
        qk = jnp.where(incl, _dot_nt(qh, kh, HIGHEST) * decay, 0.0)
        o = _dot(eg * qh, s, HIGHEST) + _dot(qk, w, HIGHEST)
        g_last = gc[c - 1:c, :]
        s_ref[h] = jnp.exp(g_last) * s + _dot_tn(kh * jnp.exp(g_last - gc), w, HIGHEST)
        on = o * lax.rsqrt(jnp.mean(o * o, axis=-1, keepdims=True) + EPS) * og_ref[...]
        zh = z_ref[0, :, h * hd:(h + 1) * hd]
        o_ref[0, :, h * hd:(h + 1) * hd] = (on * (zh * jax.nn.sigmoid(zh))).astype(o_ref.dtype)

    @pl.when(i == pl.num_programs(1) - 1)
    def _():
        s_out_ref[0] = s_ref[...]


def _gdn(rest3, small3, buf8, s0, w8, nega_row, dtb_row, og_row, c, n_valid):
    nb, t, _ = rest3.shape
    r8 = c // SUBLANES
    qkv_w = 3 * GDN_WIDTH
    kern = functools.partial(_gdn_kernel, n_valid=n_valid)
    const2 = lambda b, i: (0, 0)
    return pl.pallas_call(
        kern,
        grid=(nb, t // c),
        in_specs=[
            pl.BlockSpec((1, c, qkv_w), lambda b, i: (b, i, REST_QKV // qkv_w)),
            pl.BlockSpec((1, c, GDN_WIDTH), lambda b, i: (b, i, REST_Z // GDN_WIDTH)),
            pl.BlockSpec((1, c, LANES), lambda b, i: (b, i, 0)),
            pl.BlockSpec((1, SUBLANES, qkv_w), lambda b, i: (b, jnp.maximum(i * r8 - 1, 0), REST_QKV // qkv_w)),
            pl.BlockSpec((1, SUBLANES, qkv_w), lambda b, i: (b, 0, 0)),
            pl.BlockSpec((1, GDN_HEADS, GDN_HEAD_DIM, GDN_HEAD_DIM), lambda b, i: (b, 0, 0, 0)),
            pl.BlockSpec((SUBLANES, qkv_w), const2),
            pl.BlockSpec((1, LANES), const2),
            pl.BlockSpec((1, LANES), const2),
            pl.BlockSpec((1, GDN_HEAD_DIM), const2),
        ],
        out_specs=[
            pl.BlockSpec((1, c, GDN_WIDTH), lambda b, i: (b, i, 0)),
            pl.BlockSpec((1, GDN_HEADS, GDN_HEAD_DIM, GDN_HEAD_DIM), lambda b, i: (b, 0, 0, 0)),
        ],
        out_shape=[
            jax.ShapeDtypeStruct((nb, t, GDN_WIDTH), BF16),
            jax.ShapeDtypeStruct((nb, GDN_HEADS, GDN_HEAD_DIM, GDN_HEAD_DIM), F32),
        ],
        scratch_shapes=[
            pltpu.VMEM((GDN_HEADS, GDN_HEAD_DIM, GDN_HEAD_DIM), F32),
            pltpu.VMEM((c + SUBLANES, qkv_w), F32),
        ],
        compiler_params=_cparams(("parallel", "arbitrary")),
        name="gdn",
    )(rest3, rest3, small3, rest3, buf8, s0, w8, nega_row, dtb_row, og_row)


def _merge_kernel(x_ref, a_ref, b_ref, c_ref, gl_ref, wpa_ref, wpb_ref, wpc_ref, wo_ref, o_ref):
    d = x_ref.shape[1]
    gl = gl_ref[...]
    mix = jax.nn.sigmoid(gl[:, 0:d]) * _dot(a_ref[...], wpa_ref[...])
    mix = mix + jax.nn.sigmoid(gl[:, d:2 * d]) * _dot(b_ref[...], wpb_ref[...])
    mix = mix + jax.nn.sigmoid(gl[:, 2 * d:3 * d]) * _dot(c_ref[...], wpc_ref[...])
    o_ref[...] = x_ref[...] + _dot(mix.astype(BF16), wo_ref[...])


def _merge(x2d, out_a, out_b, out_c, rest, wpa, wpb, wpc, wo, tm):
    n, d = x2d.shape
    row = lambda i: (i, 0)
    const = lambda i: (0, 0)
    return pl.pallas_call(
        _merge_kernel,
        grid=(n // tm,),
        in_specs=[
            pl.BlockSpec((tm, d), row),
            pl.BlockSpec((tm, COL_TILE), row),
            pl.BlockSpec((tm, COL_TILE), row),
            pl.BlockSpec((tm, COL_TILE), row),
            pl.BlockSpec((tm, N_BRANCH * d), row),
            pl.BlockSpec((COL_TILE, d), const),
            pl.BlockSpec((COL_TILE, d), const),
            pl.BlockSpec((COL_TILE, d), const),
            pl.BlockSpec((d, d), const),
        ],
        out_specs=pl.BlockSpec((tm, d), row),
        out_shape=jax.ShapeDtypeStruct((n, d), F32),
        compiler_params=_cparams(("parallel",)),
        name="merge",
    )(x2d, out_a, out_b, out_c, rest, wpa, wpb, wpc, wo)


def _mlp_kernel(x_ref, g_ref, wu_ref, wd_ref, o_ref, h_ref, acc_ref):
    j = pl.program_id(1)

    @pl.when(j == 0)
    def _():
        x = x_ref[...]
        ms = jnp.mean(x * x, axis=-1, keepdims=True)
        h_ref[...] = (x * lax.rsqrt(ms + EPS) * g_ref[...]).astype(BF16)
        acc_ref[...] = x

    u = jnp.maximum(_dot(h_ref[...], wu_ref[...]), 0.0)
    acc_ref[...] += _dot((u * u).astype(BF16), wd_ref[...])

    @pl.when(j == pl.num_programs(1) - 1)
    def _():
        o_ref[...] = acc_ref[...]


def _mlp(x2d, g, w_up, w_down, tm, tf):
    n, d = x2d.shape
    dff = w_up.shape[1]
    return pl.pallas_call(
        _mlp_kernel,
        grid=(n // tm, dff // tf),
        in_specs=[
            pl.BlockSpec((tm, d), lambda i, j: (i, 0)),
            pl.BlockSpec((1, d), lambda i, j: (0, 0)),
            pl.BlockSpec((d, tf), lambda i, j: (0, j)),
            pl.BlockSpec((tf, d), lambda i, j: (j, 0)),
        ],
        out_specs=pl.BlockSpec((tm, d), lambda i, j: (i, 0)),
        out_shape=jax.ShapeDtypeStruct((n, d), F32),
        scratch_shapes=[pltpu.VMEM((tm, d), BF16), pltpu.VMEM((tm, d), F32)],
        compiler_params=_cparams(("parallel", "arbitrary")),
        name="mlp",
    )(x2d, g, w_up, w_down)


def _pad_rows_front(a, rows):
    pad = rows - a.shape[1]
    return jnp.pad(a, ((0, 0), (pad, 0), (0, 0)))


def _split3_bf16(v):
    hi = v.astype(BF16)
    mid = (v - hi.astype(F32)).astype(BF16)
    lo = (v - hi.astype(F32) - mid.astype(F32)).astype(BF16)
    return hi, mid, lo


def _layer_weights(l, norm1_g, w_in, qn_g, kn_g, sb_bias, w_conv_b, w_conv_c, a_log, dt_bias, onorm_g,
                   w_pa, w_pb, w_pc, w_o, norm2_g, w_up, w_down):
    d = w_in.shape[1]
    main_w = 3 * SB_WIDTH + 3 * SC_WIDTH + 4 * GDN_WIDTH
    small_w = 2 * GDN_HEADS
    wl = w_in[l]
    w_main = jnp.concatenate([wl[:, :3 * SB_WIDTH], wl[:, main_w + small_w:], wl[:, 3 * SB_WIDTH:main_w]],
                             axis=1).astype(BF16)
    w_small = jnp.pad(wl[:, main_w:main_w + small_w], ((0, 0), (0, LANES - small_w))).astype(BF16)
    head_of = jnp.arange(SB_WIDTH) // SB_HEAD_DIM
    pmat = (head_of[:, None] == head_of[None, :]).astype(F32) / SB_HEAD_DIM
    bias2 = sb_bias[l].astype(F32) * LOG2E
    nega = -jnp.exp(a_log[l].astype(F32))
    zeros4 = jnp.zeros((GDN_HEADS,), F32)
    pad_lanes = lambda v: jnp.pad(v, (0, LANES - v.shape[0]))[None, :]
    return dict(
        g1=norm1_g[l][None, :], w_main=w_main, w_small=w_small,
        qg=jnp.tile(qn_g[l], SB_HEADS)[None, :], kg=jnp.tile(kn_g[l], SB_HEADS)[None, :],
        pmat=pmat.astype(BF16), bias2=bias2,
        wcb=jnp.pad(w_conv_b[l], ((0, SUBLANES - SC_TAPS), (0, 0))),
        wcc=jnp.pad(w_conv_c[l], ((0, SUBLANES - GDN_TAPS), (0, 0))),
        nega_row=pad_lanes(jnp.concatenate([zeros4, nega])),
        dtb_row=pad_lanes(jnp.concatenate([zeros4, dt_bias[l].astype(F32)])),
        og_row=onorm_g[l][None, :],
        wpa=w_pa[l].astype(BF16), wpb=w_pb[l].astype(BF16), wpc=w_pc[l].astype(BF16),
        wo=w_o[l].astype(BF16), g2=norm2_g[l][None, :],
        w_up=w_up[l].astype(BF16), w_down=w_down[l].astype(BF16),
    )


def _prompt_attention(q, k, v, bias2):
    nb, t, _ = q.shape
    to_heads = lambda a: a.reshape(nb, t, SB_HEADS, SB_HEAD_DIM).transpose(0, 2, 1, 3)
    hi, mid, lo = _split3_bf16(bias2)
    bias_cols = jnp.stack([hi, mid, lo], axis=-1)
    bias_cols = jnp.broadcast_to(bias_cols[None, :, None, :], (nb, SB_HEADS, t, 3))
    zpad = jnp.zeros((nb, SB_HEADS, t, LANES - SB_HEAD_DIM - 3), BF16)
    q_aug = jnp.concatenate([to_heads(q), bias_cols, zpad], axis=-1)
    k_aug = jnp.concatenate([to_heads(k.astype(BF16)), jnp.ones((nb, SB_HEADS, t, 3), BF16), zpad], axis=-1)
    v_h = to_heads(v.astype(BF16))
    idx = jnp.arange(SB_TK)
    umat = (idx[:, None] >= idx[None, :]).astype(BF16)
    bh = nb * SB_HEADS
    o = _sb_prompt(q_aug.reshape(bh, t, LANES), k_aug.reshape(bh, t, LANES),
                   v_h.reshape(bh, t, SB_HEAD_DIM), umat)
    return o.reshape(nb, SB_HEADS, t, SB_HEAD_DIM).transpose(0, 2, 1, 3).reshape(nb * t, SB_WIDTH)


def _decode_attention(q, k, v, bias2, page_ids, cache_k2, cache_v2):
    col = jnp.arange(LANES)
    ncol = SB_HEADS * DEC_TQ
    q_rep = jnp.tile(q.transpose(0, 2, 1), (1, 1, LANES // DEC_TQ))
    same_head = (jnp.arange(SB_WIDTH)[:, None] // SB_HEAD_DIM) == (col[None, :] // DEC_TQ)
    qbd = jnp.where(same_head[None], q_rep, jnp.zeros((), BF16))
    bias_row = jnp.where(col < ncol, bias2[jnp.minimum(col // DEC_TQ, SB_HEADS - 1)], 0.0)[None, :]
    sel = ((col[None, :] % DEC_TQ == jnp.arange(DEC_TQ)[:, None]) & (col[None, :] < ncol)).astype(F32)
    idx = jnp.arange(PAGE_SIZE)
    lmat = (idx[None, :] >= idx[:, None]).astype(BF16)
    pad_page = lambda a: jnp.pad(a, ((0, 0), (0, PAGE_SIZE - a.shape[1]), (0, 0)))
    o = _sb_decode(page_ids, qbd, pad_page(k), pad_page(v), bias_row, sel, lmat, cache_k2, cache_v2)
    return o.reshape(-1, SB_WIDTH)


def _trunk_layer(x3, w, attend, buf_b, buf_c, s0, chunk, n_valid):
    nb, t, d = x3.shape
    n = nb * t
    tm = min(512, n)
    q, k, v, rest, small = _proj(x3.reshape(n, d), w["g1"], w["w_main"], w["w_small"],
                                 w["qg"], w["kg"], w["pmat"], tm)
    out_a = attend(q.reshape(nb, t, SB_WIDTH), k.reshape(nb, t, SB_WIDTH), v.reshape(nb, t, SB_WIDTH))
    rest3 = rest.reshape(nb, t, rest.shape[1])
    out_b, tail_b = _convb(rest3, _pad_rows_front(buf_b, SUBLANES), w["wcb"], min(512, t))
    out_c, s_new = _gdn(rest3, small.reshape(nb, t, LANES), _pad_rows_front(buf_c, SUBLANES), s0,
                        w["wcc"], w["nega_row"], w["dtb_row"], w["og_row"], chunk, n_valid)
    x1 = _merge(x3.reshape(n, d), out_a, out_b.reshape(n, SC_WIDTH), out_c.reshape(n, GDN_WIDTH), rest,
                w["wpa"], w["wpb"], w["wpc"], w["wo"], tm)
    x2 = _mlp(x1, w["g2"], w["w_up"], w["w_down"], tm, 1024)
    nv = n_valid - (t - SUBLANES)
    new_b = tail_b[:, nv - (SC_TAPS - 1):nv]
    new_c = rest3[:, n_valid - (GDN_TAPS - 1):n_valid, REST_QKV:REST_QKV + 3 * GDN_WIDTH]
    return x2.reshape(nb, t, d), k, v, new_b, new_c, s_new


def kernel(x_prompt, x_sample, cache_k, cache_v, page_table, state_conv_b, state_conv_c, state_gdn,
           norm1_g, w_in, qn_g, kn_g, sb_bias, w_conv_b, w_conv_c, a_log, dt_bias, onorm_g,
           w_pa, w_pb, w_pc, w_o, norm2_g, w_up, w_down):
    depth = w_in.shape[0]
    nbp, tp, d = x_prompt.shape
    nbs, ts, _ = x_sample.shape
    n_phys = cache_k.shape[1]
    assert ts <= DEC_TQ and ts >= GDN_TAPS - 1 and tp % SB_TQ == 0
    assert page_table.shape[1] % DEC_PAGES_PER_STEP == 0 and cache_k.shape[2] == PAGE_SIZE

    xp = x_prompt
    xs = jnp.pad(x_sample, ((0, 0), (0, DEC_TQ - ts), (0, 0)))
    cache_k2 = cache_k.reshape(depth * n_phys, PAGE_SIZE, SB_WIDTH)
    cache_v2 = cache_v.reshape(depth * n_phys, PAGE_SIZE, SB_WIDTH)

    outs_p, outs_s = [], []
    for l in range(depth):
        w = _layer_weights(l, norm1_g, w_in, qn_g, kn_g, sb_bias, w_conv_b, w_conv_c, a_log, dt_bias,
                           onorm_g, w_pa, w_pb, w_pc, w_o, norm2_g, w_up, w_down)
        zb = jnp.zeros((nbp, SC_TAPS - 1, SC_WIDTH), F32)
        zc = jnp.zeros((nbp, GDN_TAPS - 1, 3 * GDN_WIDTH), F32)
        zs = jnp.zeros((nbp, GDN_HEADS, GDN_HEAD_DIM, GDN_HEAD_DIM), F32)
        attend_p = functools.partial(_prompt_attention, bias2=w["bias2"])
        xp, k_p, v_p, cb_p, cc_p, s_p = _trunk_layer(xp, w, attend_p, zb, zc, zs, GDN_CHUNK, tp)
        page_ids = page_table.astype(jnp.int32) + l * n_phys
        attend_s = functools.partial(_decode_attention, bias2=w["bias2"], page_ids=page_ids,
                                     cache_k2=cache_k2, cache_v2=cache_v2)
        xs, k_s, v_s, cb_s, cc_s, s_s = _trunk_layer(xs, w, attend_s, state_conv_b[l], state_conv_c[l],
                                                     state_gdn[l], DEC_TQ, ts)
        outs_p.append((k_p.reshape(nbp, tp, SB_HEADS, SB_HEAD_DIM), v_p.reshape(nbp, tp, SB_HEADS, SB_HEAD_DIM),
                       cb_p, cc_p, s_p))
        outs_s.append((k_s.reshape(nbs, DEC_TQ, SB_HEADS, SB_HEAD_DIM)[:, :ts],
                       v_s.reshape(nbs, DEC_TQ, SB_HEADS, SB_HEAD_DIM)[:, :ts], cb_s, cc_s, s_s))

    stack = lambda outs, i: jnp.stack([o[i] for o in outs])
    return (xp, xs[:, :ts],
            stack(outs_p, 0), stack(outs_p, 1), stack(outs_p, 2), stack(outs_p, 3), stack(outs_p, 4),
            stack(outs_s, 0), stack(outs_s, 1), stack(outs_s, 2), stack(outs_s, 3), stack(outs_s, 4))
```

```python
import functools
import math

import jax
import jax.numpy as jnp
from jax import lax
from jax.experimental import pallas as pl
from jax.experimental.pallas import tpu as pltpu

F32 = jnp.float32
BF16 = jnp.bfloat16
HIGHEST = lax.Precision.HIGHEST

EPS = 1e-6
LOG2E = 1.4426950408889634

SB_HEADS = 8
SB_HEAD_DIM = 64
SB_WIDTH = SB_HEADS * SB_HEAD_DIM
SC_WIDTH = 512
SC_TAPS = 3
GDN_HEADS = 4
GDN_HEAD_DIM = 128
GDN_WIDTH = GDN_HEADS * GDN_HEAD_DIM
GDN_TAPS = 4
GDN_CHUNK = 64
GDN_ROWS_PER_STEP = 4 * GDN_CHUNK
PAGE_SIZE = 128
N_BRANCH = 3
D_MODEL = 1024

REST_GATES = 0
REST_BCX = REST_GATES + N_BRANCH * D_MODEL
REST_QKV = REST_BCX + 3 * SC_WIDTH
REST_Z = REST_QKV + 3 * GDN_WIDTH

LANES = 128
SUBLANES = 8
COL_TILE = 512
VMEM_LIMIT = 48 * 1024 * 1024

SB_TQ = 512
SB_TK = 256
MASKED_LOGIT = -1e30
DEC_TQ = 8
DEC_PAGES_PER_STEP = 8


def _cparams(sem):
    return pltpu.CompilerParams(dimension_semantics=sem, vmem_limit_bytes=VMEM_LIMIT)


def _dot(a, b, precision=None):
    return jnp.dot(a, b, preferred_element_type=F32, precision=precision)


def _dot_nt(a, b, precision=None):
    return lax.dot_general(a, b, (((1,), (1,)), ((), ())), preferred_element_type=F32, precision=precision)


def _dot_tn(a, b, precision=None):
    return lax.dot_general(a, b, (((0,), (0,)), ((), ())), preferred_element_type=F32, precision=precision)


def _softplus2(z):
    return jnp.maximum(z, 0.0) + jnp.log(1.0 + jnp.exp2(-jnp.abs(z))) * LOG2E


def _proj_kernel(x_ref, g_ref, w_ref, ws_ref, qg_ref, kg_ref, p_ref,
                 q_ref, k_ref, v_ref, rest_ref, small_ref, h_ref, *, q_scale):
    j = pl.program_id(1)

    @pl.when(j == 0)
    def _():
        x = x_ref[...]
        ms = jnp.mean(x * x, axis=-1, keepdims=True)
        h = (x * lax.rsqrt(ms + EPS) * g_ref[...]).astype(BF16)
        h_ref[...] = h
        small_ref[...] = _dot(h, ws_ref[...])

    y = _dot(h_ref[...], w_ref[...])

    def head_norm(y, gain):
        sq = y * y
        hi = sq.astype(BF16)
        lo = (sq - hi.astype(F32)).astype(BF16)
        ms = _dot(hi, p_ref[...]) + _dot(lo, p_ref[...])
        return y * lax.rsqrt(ms + EPS) * gain

    @pl.when(j == 0)
    def _():
        q_ref[...] = (head_norm(y, qg_ref[...]) * q_scale).astype(BF16)

    @pl.when(j == 1)
    def _():
        k_ref[...] = head_norm(y, kg_ref[...])

    @pl.when(j == 2)
    def _():
        v_ref[...] = y

    @pl.when(j >= 3)
    def _():
        rest_ref[...] = y


def _proj(x2d, g, w_main, w_small, qg, kg, pmat, tm):
    n, d = x2d.shape
    ncol = w_main.shape[1] // COL_TILE
    nrest = w_main.shape[1] - 3 * COL_TILE
    kern = functools.partial(_proj_kernel, q_scale=SB_HEAD_DIM ** -0.5 * LOG2E)
    row = lambda i, j: (i, 0)
    const = lambda i, j: (0, 0)
    return pl.pallas_call(
        kern,
        grid=(n // tm, ncol),
        in_specs=[
            pl.BlockSpec((tm, d), row),
            pl.BlockSpec((1, d), const),
            pl.BlockSpec((d, COL_TILE), lambda i, j: (0, j)),
            pl.BlockSpec((d, LANES), const),
            pl.BlockSpec((1, COL_TILE), const),
            pl.BlockSpec((1, COL_TILE), const),
            pl.BlockSpec((COL_TILE, COL_TILE), const),
        ],
        out_specs=[
            pl.BlockSpec((tm, COL_TILE), row),
            pl.BlockSpec((tm, COL_TILE), row),
            pl.BlockSpec((tm, COL_TILE), row),
            pl.BlockSpec((tm, COL_TILE), lambda i, j: (i, jnp.maximum(j - 3, 0))),
            pl.BlockSpec((tm, LANES), row),
        ],
        out_shape=[
            jax.ShapeDtypeStruct((n, COL_TILE), BF16),
            jax.ShapeDtypeStruct((n, COL_TILE), F32),
            jax.ShapeDtypeStruct((n, COL_TILE), F32),
            jax.ShapeDtypeStruct((n, nrest), F32),
            jax.ShapeDtypeStruct((n, LANES), F32),
        ],
        scratch_shapes=[pltpu.VMEM((tm, d), BF16)],
        compiler_params=_cparams(("parallel", "arbitrary")),
        name="proj",
    )(x2d, g, w_main, w_small, qg, kg, pmat)


def _sb_prompt_kernel(q_ref, k_ref, v_ref, u_ref, o_ref, z_ref, sp_ref, a_ref, m0_ref, acc_ref, r_ref):
    qi = pl.program_id(1)
    tq = q_ref.shape[1]
    tk = u_ref.shape[0]
    ratio = tq // tk
    n_tiles = ratio * (qi + 1)

    def key_start(n):
        return pl.multiple_of((n_tiles - 1 - n) * tk, tk)

    def scores(n, slot, diag):
        kt = k_ref[0, :, pl.ds(key_start(n), tk)]
        z = _dot(q_ref[0], kt)
        sp = _softplus2(z)
        if diag:
            col = lax.broadcasted_iota(jnp.int32, (tq, tk), 1) + (tq - (n + 1) * tk)
            valid = col < lax.broadcasted_iota(jnp.int32, (tq, tk), 0)
            sp = jnp.where(valid, sp, 0.0)
            z = jnp.where(valid, z, MASKED_LOGIT)
        z_ref[slot] = z
        sp_ref[slot] = sp.astype(BF16)

    def weights(slot):
        m = _dot(sp_ref[slot], u_ref[...])
        a_ref[slot] = jnp.exp2(z_ref[slot] - m).astype(BF16)
        m0_ref[slot] = m[:, 0:1]

    def accumulate(n, slot):
        vt = v_ref[0, pl.ds(key_start(n), tk), :]
        pv = _dot(a_ref[slot], vt)
        r = r_ref[...]
        acc_ref[...] += jnp.exp2(r) * pv
        r_ref[...] = r - m0_ref[slot]

    acc_ref[...] = jnp.zeros_like(acc_ref)
    r_ref[...] = jnp.zeros_like(r_ref)
    scores(0, 0, True)
    scores(1, 1, True)
    weights(0)

    def body(p, carry):
        s = 2 * p
        accumulate(s, 0)
        weights(1)
        scores(s + 2, 0, False)
        accumulate(s + 1, 1)
        weights(0)
        scores(s + 3, 1, False)
        return carry

    lax.fori_loop(0, qi, body, 0)
    accumulate(n_tiles - 2, 0)
    weights(1)
    accumulate(n_tiles - 1, 1)
    o_ref[0] = acc_ref[...].astype(o_ref.dtype)


def _sb_prompt(q_aug, k_aug, v_h, umat):
    bh, t, _ = q_aug.shape
    assert SB_TQ == 2 * SB_TK
    return pl.pallas_call(
        _sb_prompt_kernel,
        grid=(bh, t // SB_TQ),
        in_specs=[
            pl.BlockSpec((1, SB_TQ, LANES), lambda b, i: (b, i, 0)),
            pl.BlockSpec((1, LANES, t), lambda b, i: (b, 0, 0)),
            pl.BlockSpec((1, t, SB_HEAD_DIM), lambda b, i: (b, 0, 0)),
            pl.BlockSpec((SB_TK, SB_TK), lambda b, i: (0, 0)),
        ],
        out_specs=pl.BlockSpec((1, SB_TQ, SB_HEAD_DIM), lambda b, i: (b, i, 0)),
        out_shape=jax.ShapeDtypeStruct((bh, t, SB_HEAD_DIM), BF16),
        scratch_shapes=[
            pltpu.VMEM((2, SB_TQ, SB_TK), F32),
            pltpu.VMEM((2, SB_TQ, SB_TK), BF16),
            pltpu.VMEM((2, SB_TQ, SB_TK), BF16),
            pltpu.VMEM((2, SB_TQ, 1), F32),
            pltpu.VMEM((SB_TQ, SB_HEAD_DIM), F32),
            pltpu.VMEM((SB_TQ, 1), F32),
        ],
        compiler_params=_cparams(("parallel", "arbitrary")),
        name="sb_prompt",
    )(q_aug, k_aug, v_h, umat)


def _sb_decode_kernel(pt_ref, qbd_ref, kn_ref, vn_ref, bias_ref, sel_ref, lmat_ref, *rest):
    del pt_ref
    npg = DEC_PAGES_PER_STEP
    kp_refs = rest[:npg]
    vp_refs = rest[npg:2 * npg]
    o_ref, acc_ref, r_ref = rest[2 * npg:]
    g = pl.program_id(1)

    def block(kb, vb, valid):
        z = _dot(kb.astype(BF16), qbd_ref[0]) + bias_ref[...]
        sp = _softplus2(z)
        if valid is not None:
            sp = jnp.where(valid, sp, 0.0)
        m = _dot(lmat_ref[...], sp.astype(BF16))
        a = jnp.exp2(z - m + r_ref[...])
        if valid is not None:
            a = jnp.where(valid, a, 0.0)
        acc_ref[...] += _dot_tn(a.astype(BF16), vb.astype(BF16))
        r_ref[...] -= m[0:1, :]

    @pl.when(g == 0)
    def _():
        acc_ref[...] = jnp.zeros_like(acc_ref)
        r_ref[...] = jnp.zeros_like(r_ref)
        key = lax.broadcasted_iota(jnp.int32, (PAGE_SIZE, LANES), 0)
        qry = lax.broadcasted_iota(jnp.int32, (PAGE_SIZE, LANES), 1) % DEC_TQ
        block(kn_ref[0], vn_ref[0], key < qry)

    for i in reversed(range(npg)):
        block(kp_refs[i][0], vp_refs[i][0], None)

    @pl.when(g == pl.num_programs(1) - 1)
    def _():
        acc = acc_ref[...]
        rhead = lax.broadcasted_iota(jnp.int32, acc.shape, 0) // DEC_TQ
        chead = lax.broadcasted_iota(jnp.int32, acc.shape, 1) // SB_HEAD_DIM
        picked = jnp.where(rhead == chead, acc, 0.0)
        o_ref[0] = _dot(sel_ref[...], picked, HIGHEST).astype(o_ref.dtype)


def _sb_decode(page_ids, qbd, k_new, v_new, bias_row, sel, lmat, cache_k2, cache_v2):
    nb, n_pages = page_ids.shape
    npg = DEC_PAGES_PER_STEP
    ngroups = n_pages // npg

    def page_spec(i):
        return pl.BlockSpec((1, PAGE_SIZE, SB_WIDTH),
                            lambda b, g, pt: (pt[b, (ngroups - 1 - g) * npg + i], 0, 0))

    per_b = lambda b, g, pt: (b, 0, 0)
    const = lambda b, g, pt: (0, 0)
    grid_spec = pltpu.PrefetchScalarGridSpec(
        num_scalar_prefetch=1,
        grid=(nb, ngroups),
        in_specs=[
            pl.BlockSpec((1, SB_WIDTH, LANES), per_b),
            pl.BlockSpec((1, PAGE_SIZE, SB_WIDTH), per_b),
            pl.BlockSpec((1, PAGE_SIZE, SB_WIDTH), per_b),
            pl.BlockSpec((1, LANES), const),
            pl.BlockSpec((DEC_TQ, LANES), const),
            pl.BlockSpec((PAGE_SIZE, PAGE_SIZE), const),
        ] + [page_spec(i) for i in range(npg)] * 2,
        out_specs=pl.BlockSpec((1, DEC_TQ, SB_WIDTH), per_b),
        scratch_shapes=[
            pltpu.VMEM((LANES, SB_WIDTH), F32),
            pltpu.VMEM((1, LANES), F32),
        ],
    )
    return pl.pallas_call(
        _sb_decode_kernel,
        grid_spec=grid_spec,
        out_shape=jax.ShapeDtypeStruct((nb, DEC_TQ, SB_WIDTH), BF16),
        compiler_params=_cparams(("parallel", "arbitrary")),
        name="sb_decode",
    )(page_ids, qbd, k_new, v_new, bias_row, sel, lmat, *([cache_k2] * npg), *([cache_v2] * npg))


def _convb_kernel(b_ref, c_ref, x_ref, cp_ref, xp_ref, buf_ref, w_ref, o_ref, tail_ref, u_ref):
    i = pl.program_id(1)
    tt = b_ref.shape[1]
    u = c_ref[0] * x_ref[0]
    prev = jnp.where(i == 0, buf_ref[0], cp_ref[0] * xp_ref[0])
    u_ref[0:SUBLANES, :] = prev
    u_ref[SUBLANES:, :] = u
    base = SUBLANES - (SC_TAPS - 1)
    y = w_ref[0:1, :] * u_ref[base:base + tt, :]
    for j in range(1, SC_TAPS):
        y = y + w_ref[j:j + 1, :] * u_ref[base + j:base + j + tt, :]
    o_ref[0] = (b_ref[0] * y).astype(o_ref.dtype)
    tail_ref[0] = u[tt - SUBLANES:, :]


def _convb(rest3, buf8, w8, tt):
    nb, t, _ = rest3.shape
    r8 = tt // SUBLANES
    c0 = REST_BCX // COL_TILE
    cur = lambda c: pl.BlockSpec((1, tt, COL_TILE), lambda b, i: (b, i, c))
    prev = lambda c: pl.BlockSpec((1, SUBLANES, COL_TILE), lambda b, i: (b, jnp.maximum(i * r8 - 1, 0), c))
    return pl.pallas_call(
        _convb_kernel,
        grid=(nb, t // tt),
        in_specs=[cur(c0), cur(c0 + 1), cur(c0 + 2), prev(c0 + 1), prev(c0 + 2),
                  pl.BlockSpec((1, SUBLANES, COL_TILE), lambda b, i: (b, 0, 0)),
                  pl.BlockSpec((SUBLANES, COL_TILE), lambda b, i: (0, 0))],
        out_specs=[pl.BlockSpec((1, tt, COL_TILE), lambda b, i: (b, i, 0)),
                   pl.BlockSpec((1, SUBLANES, COL_TILE), lambda b, i: (b, 0, 0))],
        out_shape=[jax.ShapeDtypeStruct((nb, t, COL_TILE), BF16),
                   jax.ShapeDtypeStruct((nb, SUBLANES, COL_TILE), F32)],
        scratch_shapes=[pltpu.VMEM((tt + SUBLANES, COL_TILE), F32)],
        compiler_params=_cparams(("parallel", "arbitrary")),
        name="convb",
    )(rest3, rest3, rest3, rest3, rest3, buf8, w8)


def _split_bf16(x):
    hi = x.astype(BF16)
    return hi, (x - hi.astype(F32)).astype(BF16)


def _dot3(a, b, nt=False):
    ah, al = _split_bf16(a)
    bh, bl = _split_bf16(b)
    d = _dot_nt if nt else _dot
    return d(ah, bh) + (d(al, bh) + d(ah, bl))


def _dotb(a, b):
    return _dot(a.astype(BF16), b.astype(BF16))


def _gdn_kernel(x_ref, z_ref, sm_ref, xp_ref, buf_ref, s0_ref, w_ref, nega_ref, dtb_ref, og_ref,
                o_ref, s_out_ref, s_ref, xb_ref, *, n_valid):
    i = pl.program_id(1)
    rows = x_ref.shape[1]
    c = GDN_CHUNK
    nh = GDN_HEADS
    hd = GDN_HEAD_DIM

    @pl.when(i == 0)
    def _():
        s_ref[...] = s0_ref[0]

    xb_ref[0:SUBLANES, :] = jnp.where(i == 0, buf_ref[0], xp_ref[0])
    xb_ref[SUBLANES:, :] = x_ref[0]
    base = SUBLANES - (GDN_TAPS - 1)
    y = w_ref[0:1, :] * xb_ref[base:base + rows, :]
    for j in range(1, GDN_TAPS):
        y = y + w_ref[j:j + 1, :] * xb_ref[base + j:base + j + rows, :]
    act = y * jax.nn.sigmoid(y)

    sm = sm_ref[0]
    beta_all = jax.nn.sigmoid(sm)
    pre = sm + dtb_ref[...]
    g_all = nega_ref[...] * (jnp.maximum(pre, 0.0) + jnp.log1p(jnp.exp(-jnp.abs(pre))))
    if n_valid < rows:
        live = lax.broadcasted_iota(jnp.int32, (rows, LANES), 0) < n_valid
        beta_all = jnp.where(live, beta_all, 0.0)
        g_all = jnp.where(live, g_all, 0.0)

    rr = lax.broadcasted_iota(jnp.int32, (rows, rows), 0)
    cc = lax.broadcasted_iota(jnp.int32, (rows, rows), 1)
    gcum = _dot3(((rr // c == cc // c) & (cc <= rr)).astype(F32), g_all)

    row = lax.broadcasted_iota(jnp.int32, (c, c), 0)
    col = lax.broadcasted_iota(jnp.int32, (c, c), 1)
    incl = col <= row
    strict = col < row
    lane = lax.broadcasted_iota(jnp.int32, (c, LANES), 1)
    wide = nh * c
    eye_w = (lax.broadcasted_iota(jnp.int32, (c, wide), 0)
             == lax.broadcasted_iota(jnp.int32, (c, wide), 1) % c).astype(F32)
    same_block = (lax.broadcasted_iota(jnp.int32, (wide, wide), 0) // c
                  == lax.broadcasted_iota(jnp.int32, (wide, wide), 1) // c)

    def times_blockdiag(x, p):
        xh, xl = _split_bf16(x)
        ph, pl_ = _split_bf16(p)
        zero = jnp.zeros((), BF16)
        bdh = jnp.where(same_block, jnp.tile(ph, (nh, 1)), zero)
        bdl = jnp.where(same_block, jnp.tile(pl_, (nh, 1)), zero)
        return _dot(xh, bdh) + (_dot(xl, bdh) + _dot(xh, bdl))

    nchunks = rows // c
    chunks = []
    for ci in range(nchunks):
        r0 = ci * c
        heads = []
        for h in range(nh):
            qh = act[r0:r0 + c, h * hd:(h + 1) * hd]
            kh = act[r0:r0 + c, GDN_WIDTH + h * hd:GDN_WIDTH + (h + 1) * hd]
            vh = act[r0:r0 + c, 2 * GDN_WIDTH + h * hd:2 * GDN_WIDTH + (h + 1) * hd]
            qh = qh * lax.rsqrt(jnp.sum(qh * qh, axis=-1, keepdims=True) + EPS) * (hd ** -0.5)
            kh = kh * lax.rsqrt(jnp.sum(kh * kh, axis=-1, keepdims=True) + EPS)
            b = beta_all[r0:r0 + c, h:h + 1]
            gc = gcum[r0:r0 + c, nh + h:nh + h + 1]
            xl = jnp.where(lane == 0, gc, jnp.where(lane == 1, 1.0, 0.0))
            yl = jnp.where(lane == 0, 1.0, jnp.where(lane == 1, -gc, 0.0))
            d = _dot3(xl, yl, nt=True)
            decay = jnp.where(incl, jnp.exp(jnp.where(incl, d, 0.0)), 0.0)
            a = jnp.where(strict, b * decay * _dot3(kh, kh, nt=True), 0.0)
            heads.append((qh, kh, vh, b, gc, decay, a))
        chunks.append(heads)

    ps = [jnp.concatenate([hh[6] for hh in heads], axis=1) for heads in chunks]
    tinvs = [eye_w - p for p in ps]
    for _ in range(int(math.log2(c)) - 1):
        ps = [times_blockdiag(p, p) for p in ps]
        tinvs = [t + times_blockdiag(t, p) for t, p in zip(tinvs, ps)]

    prepared = []
    for heads, tinv in zip(chunks, tinvs):
        per_head = []
        for h, (qh, kh, vh, b, gc, decay, _) in enumerate(heads):
            th = tinv[:, h * c:(h + 1) * c]
            eg = jnp.exp(gc)
            g_last = gc[c - 1:c, :]
            u = _dotb(th, b * vh)
            wk = _dotb(th, (b * eg) * kh).astype(BF16)
            qk = jnp.where(incl, _dot_nt(qh.astype(BF16), kh.astype(BF16)) * decay, 0.0).astype(BF16)
            per_head.append((u, wk, qk, (eg * qh).astype(BF16),
                             (kh * jnp.exp(g_last - gc)).astype(BF16), jnp.exp(g_last)))
        prepared.append(per_head)

    state = [s_ref[h] for h in range(nh)]
    for ci, per_head in enumerate(prepared):
        r0 = ci * c
        for h, (u, wk, qk, egq, kd, dec_last) in enumerate(per_head):
            s = state[h]
            sb = s.astype(BF16)
            w = u - _dot(wk, sb)
            wb = w.astype(BF16)
            o = _dot(egq, sb) + _dot(qk, wb)
            state[h] = dec_last * s + _dot_tn(kd, wb)
            on = o * lax.rsqrt(jnp.mean(o * o, axis=-1, keepdims=True) + EPS) * og_ref[...]
            zh = z_ref[0, r0:r0 + c, h * hd:(h + 1) * hd]
            o_ref[0, r0:r0 + c, h * hd:(h + 1) * hd] = (on * (zh * jax.nn.sigmoid(zh))).astype(o_ref.dtype)
    for h in range(nh):
        s_ref[h] = state[h]

    @pl.when(i == pl.num_programs(1) - 1)
    def _():
        s_out_ref[0] = s_ref[...]


def _gdn(rest3, small3, buf8, s0, w8, nega_row, dtb_row, og_row, c, n_valid):
    nb, t, _ = rest3.shape
    r8 = c // SUBLANES
    qkv_w = 3 * GDN_WIDTH
    kern = functools.partial(_gdn_kernel, n_valid=n_valid)
    const2 = lambda b, i: (0, 0)
    return pl.pallas_call(
        kern,
        grid=(nb, t // c),
        in_specs=[
            pl.BlockSpec((1, c, qkv_w), lambda b, i: (b, i, REST_QKV // qkv_w)),
            pl.BlockSpec((1, c, GDN_WIDTH), lambda b, i: (b, i, REST_Z // GDN_WIDTH)),
            pl.BlockSpec((1, c, LANES), lambda b, i: (b, i, 0)),
            pl.BlockSpec((1, SUBLANES, qkv_w), lambda b, i: (b, jnp.maximum(i * r8 - 1, 0), REST_QKV // qkv_w)),
            pl.BlockSpec((1, SUBLANES, qkv_w), lambda b, i: (b, 0, 0)),
            pl.BlockSpec((1, GDN_HEADS, GDN_HEAD_DIM, GDN_HEAD_DIM), lambda b, i: (b, 0, 0, 0)),
            pl.BlockSpec((SUBLANES, qkv_w), const2),
            pl.BlockSpec((1, LANES), const2),
            pl.BlockSpec((1, LANES), const2),
            pl.BlockSpec((1, GDN_HEAD_DIM), const2),
        ],
        out_specs=[
            pl.BlockSpec((1, c, GDN_WIDTH), lambda b, i: (b, i, 0)),
            pl.BlockSpec((1, GDN_HEADS, GDN_HEAD_DIM, GDN_HEAD_DIM), lambda b, i: (b, 0, 0, 0)),
        ],
        out_shape=[
            jax.ShapeDtypeStruct((nb, t, GDN_WIDTH), BF16),
            jax.ShapeDtypeStruct((nb, GDN_HEADS, GDN_HEAD_DIM, GDN_HEAD_DIM), F32),
        ],
        scratch_shapes=[
            pltpu.VMEM((GDN_HEADS, GDN_HEAD_DIM, GDN_HEAD_DIM), F32),
            pltpu.VMEM((c + SUBLANES, qkv_w), F32),
        ],
        compiler_params=_cparams(("parallel", "arbitrary")),
        name="gdn",
    )(rest3, rest3, small3, rest3, buf8, s0, w8, nega_row, dtb_row, og_row)


def _merge_kernel(x_ref, a_ref, b_ref, c_ref, gl_ref, wpa_ref, wpb_ref, wpc_ref, wo_ref, o_ref):
    d = x_ref.shape[1]
    gl = gl_ref[...]
    mix = jax.nn.sigmoid(gl[:, 0:d]) * _dot(a_ref[...], wpa_ref[...])
    mix = mix + jax.nn.sigmoid(gl[:, d:2 * d]) * _dot(b_ref[...], wpb_ref[...])
    mix = mix + jax.nn.sigmoid(gl[:, 2 * d:3 * d]) * _dot(c_ref[...], wpc_ref[...])
    o_ref[...] = x_ref[...] + _dot(mix.astype(BF16), wo_ref[...])


def _merge(x2d, out_a, out_b, out_c, rest, wpa, wpb, wpc, wo, tm):
    n, d = x2d.shape
    row = lambda i: (i, 0)
    const = lambda i: (0, 0)
    return pl.pallas_call(
        _merge_kernel,
        grid=(n // tm,),
        in_specs=[
            pl.BlockSpec((tm, d), row),
            pl.BlockSpec((tm, COL_TILE), row),
            pl.BlockSpec((tm, COL_TILE), row),
            pl.BlockSpec((tm, COL_TILE), row),
            pl.BlockSpec((tm, N_BRANCH * d), row),
            pl.BlockSpec((COL_TILE, d), const),
            pl.BlockSpec((COL_TILE, d), const),
            pl.BlockSpec((COL_TILE, d), const),
            pl.BlockSpec((d, d), const),
        ],
        out_specs=pl.BlockSpec((tm, d), row),
        out_shape=jax.ShapeDtypeStruct((n, d), F32),
        compiler_params=_cparams(("parallel",)),
        name="merge",
    )(x2d, out_a, out_b, out_c, rest, wpa, wpb, wpc, wo)


def _mlp_kernel(x_ref, g_ref, wu_ref, wd_ref, o_ref, h_ref, acc_ref):
    j = pl.program_id(1)

    @pl.when(j == 0)
    def _():
        x = x_ref[...]
        ms = jnp.mean(x * x, axis=-1, keepdims=True)
        h_ref[...] = (x * lax.rsqrt(ms + EPS) * g_ref[...]).astype(BF16)
        acc_ref[...] = x

    u = jnp.maximum(_dot(h_ref[...], wu_ref[...]), 0.0)
    acc_ref[...] += _dot((u * u).astype(BF16), wd_ref[...])

    @pl.when(j == pl.num_programs(1) - 1)
    def _():
        o_ref[...] = acc_ref[...]


def _mlp(x2d, g, w_up, w_down, tm, tf):
    n, d = x2d.shape
    dff = w_up.shape[1]
    return pl.pallas_call(
        _mlp_kernel,
        grid=(n // tm, dff // tf),
        in_specs=[
            pl.BlockSpec((tm, d), lambda i, j: (i, 0)),
            pl.BlockSpec((1, d), lambda i, j: (0, 0)),
            pl.BlockSpec((d, tf), lambda i, j: (0, j)),
            pl.BlockSpec((tf, d), lambda i, j: (j, 0)),
        ],
        out_specs=pl.BlockSpec((tm, d), lambda i, j: (i, 0)),
        out_shape=jax.ShapeDtypeStruct((n, d), F32),
        scratch_shapes=[pltpu.VMEM((tm, d), BF16), pltpu.VMEM((tm, d), F32)],
        compiler_params=_cparams(("parallel", "arbitrary")),
        name="mlp",
    )(x2d, g, w_up, w_down)


def _pad_rows_front(a, rows):
    pad = rows - a.shape[1]
    return jnp.pad(a, ((0, 0), (pad, 0), (0, 0)))


def _split3_bf16(v):
    hi = v.astype(BF16)
    mid = (v - hi.astype(F32)).astype(BF16)
    lo = (v - hi.astype(F32) - mid.astype(F32)).astype(BF16)
    return hi, mid, lo


def _layer_weights(l, norm1_g, w_in, qn_g, kn_g, sb_bias, w_conv_b, w_conv_c, a_log, dt_bias, onorm_g,
                   w_pa, w_pb, w_pc, w_o, norm2_g, w_up, w_down):
    d = w_in.shape[1]
    main_w = 3 * SB_WIDTH + 3 * SC_WIDTH + 4 * GDN_WIDTH
    small_w = 2 * GDN_HEADS
    wl = w_in[l]
    w_main = jnp.concatenate([wl[:, :3 * SB_WIDTH], wl[:, main_w + small_w:], wl[:, 3 * SB_WIDTH:main_w]],
                             axis=1).astype(BF16)
    w_small = jnp.pad(wl[:, main_w:main_w + small_w], ((0, 0), (0, LANES - small_w))).astype(BF16)
    head_of = jnp.arange(SB_WIDTH) // SB_HEAD_DIM
    pmat = (head_of[:, None] == head_of[None, :]).astype(F32) / SB_HEAD_DIM
    bias2 = sb_bias[l].astype(F32) * LOG2E
    nega = -jnp.exp(a_log[l].astype(F32))
    zeros4 = jnp.zeros((GDN_HEADS,), F32)
    pad_lanes = lambda v: jnp.pad(v, (0, LANES - v.shape[0]))[None, :]
    return dict(
        g1=norm1_g[l][None, :], w_main=w_main, w_small=w_small,
        qg=jnp.tile(qn_g[l], SB_HEADS)[None, :], kg=jnp.tile(kn_g[l], SB_HEADS)[None, :],
        pmat=pmat.astype(BF16), bias2=bias2,
        wcb=jnp.pad(w_conv_b[l], ((0, SUBLANES - SC_TAPS), (0, 0))),
        wcc=jnp.pad(w_conv_c[l], ((0, SUBLANES - GDN_TAPS), (0, 0))),
        nega_row=pad_lanes(jnp.concatenate([zeros4, nega])),
        dtb_row=pad_lanes(jnp.concatenate([zeros4, dt_bias[l].astype(F32)])),
        og_row=onorm_g[l][None, :],
        wpa=w_pa[l].astype(BF16), wpb=w_pb[l].astype(BF16), wpc=w_pc[l].astype(BF16),
        wo=w_o[l].astype(BF16), g2=norm2_g[l][None, :],
        w_up=w_up[l].astype(BF16), w_down=w_down[l].astype(BF16),
    )


def _prompt_attention(q, k, v, bias2):
    nb, t, _ = q.shape
    to_heads = lambda a: a.reshape(nb, t, SB_HEADS, SB_HEAD_DIM).transpose(0, 2, 1, 3)
    hi, mid, lo = _split3_bf16(bias2)
    bias_cols = jnp.stack([hi, mid, lo], axis=-1)
    bias_cols = jnp.broadcast_to(bias_cols[None, :, None, :], (nb, SB_HEADS, t, 3))
    zpad = jnp.zeros((nb, SB_HEADS, t, LANES - SB_HEAD_DIM - 3), BF16)
    q_aug = jnp.concatenate([to_heads(q), bias_cols, zpad], axis=-1)
    k_t = k.astype(BF16).reshape(nb, t, SB_HEADS, SB_HEAD_DIM).transpose(0, 2, 3, 1)
    k_aug = jnp.concatenate([k_t, jnp.ones((nb, SB_HEADS, 3, t), BF16),
                             jnp.zeros((nb, SB_HEADS, LANES - SB_HEAD_DIM - 3, t), BF16)], axis=2)
    v_h = to_heads(v.astype(BF16))
    idx = jnp.arange(SB_TK)
    umat = (idx[:, None] >= idx[None, :]).astype(BF16)
    bh = nb * SB_HEADS
    o = _sb_prompt(q_aug.reshape(bh, t, LANES), k_aug.reshape(bh, LANES, t),
                   v_h.reshape(bh, t, SB_HEAD_DIM), umat)
    return o.reshape(nb, SB_HEADS, t, SB_HEAD_DIM).transpose(0, 2, 1, 3).reshape(nb * t, SB_WIDTH)


def _decode_attention(q, k, v, bias2, page_ids, cache_k2, cache_v2):
    col = jnp.arange(LANES)
    ncol = SB_HEADS * DEC_TQ
    q_rep = jnp.tile(q.transpose(0, 2, 1), (1, 1, LANES // DEC_TQ))
    same_head = (jnp.arange(SB_WIDTH)[:, None] // SB_HEAD_DIM) == (col[None, :] // DEC_TQ)
    qbd = jnp.where(same_head[None], q_rep, jnp.zeros((), BF16))
    bias_row = jnp.where(col < ncol, bias2[jnp.minimum(col // DEC_TQ, SB_HEADS - 1)], 0.0)[None, :]
    sel = ((col[None, :] % DEC_TQ == jnp.arange(DEC_TQ)[:, None]) & (col[None, :] < ncol)).astype(F32)
    idx = jnp.arange(PAGE_SIZE)
    lmat = (idx[None, :] >= idx[:, None]).astype(BF16)
    pad_page = lambda a: jnp.pad(a, ((0, 0), (0, PAGE_SIZE - a.shape[1]), (0, 0)))
    o = _sb_decode(page_ids, qbd, pad_page(k), pad_page(v), bias_row, sel, lmat, cache_k2, cache_v2)
    return o.reshape(-1, SB_WIDTH)


def _trunk_layer(x3, w, attend, buf_b, buf_c, s0, chunk, n_valid):
    nb, t, d = x3.shape
    n = nb * t
    tm = min(512, n)
    q, k, v, rest, small = _proj(x3.reshape(n, d), w["g1"], w["w_main"], w["w_small"],
                                 w["qg"], w["kg"], w["pmat"], tm)
    out_a = attend(q.reshape(nb, t, SB_WIDTH), k.reshape(nb, t, SB_WIDTH), v.reshape(nb, t, SB_WIDTH))
    rest3 = rest.reshape(nb, t, rest.shape[1])
    out_b, tail_b = _convb(rest3, _pad_rows_front(buf_b, SUBLANES), w["wcb"], min(512, t))
    tg = -(-t // GDN_CHUNK) * GDN_CHUNK
    pad_t = lambda a: jnp.pad(a, ((0, 0), (0, tg - t), (0, 0)))
    out_c, s_new = _gdn(pad_t(rest3), pad_t(small.reshape(nb, t, LANES)), _pad_rows_front(buf_c, SUBLANES), s0,
                        w["wcc"], w["nega_row"], w["dtb_row"], w["og_row"], min(chunk, tg), n_valid)
    out_c = out_c[:, :t]
    x1 = _merge(x3.reshape(n, d), out_a, out_b.reshape(n, SC_WIDTH), out_c.reshape(n, GDN_WIDTH), rest,
                w["wpa"], w["wpb"], w["wpc"], w["wo"], tm)
    x2 = _mlp(x1, w["g2"], w["w_up"], w["w_down"], tm, 1024)
    nv = n_valid - (t - SUBLANES)
    new_b = tail_b[:, nv - (SC_TAPS - 1):nv]
    new_c = rest3[:, n_valid - (GDN_TAPS - 1):n_valid, REST_QKV:REST_QKV + 3 * GDN_WIDTH]
    return x2.reshape(nb, t, d), k, v, new_b, new_c, s_new


def kernel(x_prompt, x_sample, cache_k, cache_v, page_table, state_conv_b, state_conv_c, state_gdn,
           norm1_g, w_in, qn_g, kn_g, sb_bias, w_conv_b, w_conv_c, a_log, dt_bias, onorm_g,
           w_pa, w_pb, w_pc, w_o, norm2_g, w_up, w_down):
    depth = w_in.shape[0]
    nbp, tp, d = x_prompt.shape
    nbs, ts, _ = x_sample.shape
    n_phys = cache_k.shape[1]
    assert ts <= DEC_TQ and ts >= GDN_TAPS - 1 and tp % SB_TQ == 0
    assert page_table.shape[1] % DEC_PAGES_PER_STEP == 0 and cache_k.shape[2] == PAGE_SIZE

    xp = x_prompt
    xs = jnp.pad(x_sample, ((0, 0), (0, DEC_TQ - ts), (0, 0)))
    cache_k2 = cache_k.reshape(depth * n_phys, PAGE_SIZE, SB_WIDTH)
    cache_v2 = cache_v.reshape(depth * n_phys, PAGE_SIZE, SB_WIDTH)

    outs_p, outs_s = [], []
    for l in range(depth):
        w = _layer_weights(l, norm1_g, w_in, qn_g, kn_g, sb_bias, w_conv_b, w_conv_c, a_log, dt_bias,
                           onorm_g, w_pa, w_pb, w_pc, w_o, norm2_g, w_up, w_down)
        zb = jnp.zeros((nbp, SC_TAPS - 1, SC_WIDTH), F32)
        zc = jnp.zeros((nbp, GDN_TAPS - 1, 3 * GDN_WIDTH), F32)
        zs = jnp.zeros((nbp, GDN_HEADS, GDN_HEAD_DIM, GDN_HEAD_DIM), F32)
        attend_p = functools.partial(_prompt_attention, bias2=w["bias2"])
        xp, k_p, v_p, cb_p, cc_p, s_p = _trunk_layer(xp, w, attend_p, zb, zc, zs, GDN_ROWS_PER_STEP, tp)
        page_ids = page_table.astype(jnp.int32) + l * n_phys
        attend_s = functools.partial(_decode_attention, bias2=w["bias2"], page_ids=page_ids,
                                     cache_k2=cache_k2, cache_v2=cache_v2)
        xs, k_s, v_s, cb_s, cc_s, s_s = _trunk_layer(xs, w, attend_s, state_conv_b[l], state_conv_c[l],
                                                     state_gdn[l], GDN_ROWS_PER_STEP, ts)
        outs_p.append((k_p.reshape(nbp, tp, SB_HEADS, SB_HEAD_DIM), v_p.reshape(nbp, tp, SB_HEADS, SB_HEAD_DIM),
                       cb_p, cc_p, s_p))
        outs_s.append((k_s.reshape(nbs, DEC_TQ, SB_HEADS, SB_HEAD_DIM)[:, :ts],
                       v_s.reshape(nbs, DEC_TQ, SB_HEADS, SB_HEAD_DIM)[:, :ts], cb_s, cc_s, s_s))

    stack = lambda outs, i: jnp.stack([o[i] for o in outs])
    return (xp, xs[:, :ts],
            stack(outs_p, 0), stack(outs_p, 1), stack(outs_p, 2), stack(outs_p, 3), stack(outs_p, 4),
            stack(outs_s, 0), stack(outs_s, 1), stack(outs_s, 2), stack(outs_s, 3), stack(outs_s, 4))
```

```python
import functools
import math

import jax
import jax.numpy as jnp
from jax import lax
from jax.experimental import pallas as pl
from jax.experimental.pallas import tpu as pltpu

F32 = jnp.float32
BF16 = jnp.bfloat16
HIGHEST = lax.Precision.HIGHEST

EPS = 1e-6
LOG2E = 1.4426950408889634

SB_HEADS = 8
SB_HEAD_DIM = 64
SB_WIDTH = SB_HEADS * SB_HEAD_DIM
SC_WIDTH = 512
SC_TAPS = 3
GDN_HEADS = 4
GDN_HEAD_DIM = 128
GDN_WIDTH = GDN_HEADS * GDN_HEAD_DIM
GDN_TAPS = 4
GDN_CHUNK = 64
GDN_ROWS_PER_STEP = 4 * GDN_CHUNK
PAGE_SIZE = 128
N_BRANCH = 3
D_MODEL = 1024

REST_GATES = 0
REST_BCX = REST_GATES + N_BRANCH * D_MODEL
REST_QKV = REST_BCX + 3 * SC_WIDTH
REST_Z = REST_QKV + 3 * GDN_WIDTH

LANES = 128
SUBLANES = 8
COL_TILE = 512
ROW_TILE = 512
PROJ_ROW_TILE = 1024
VMEM_LIMIT = 48 * 1024 * 1024

SB_TQ = 512
SB_TK = 256
MASKED_LOGIT = -1e30
DEC_TQ = 8
DEC_PAGES_PER_STEP = 8


def _cparams(sem):
    return pltpu.CompilerParams(dimension_semantics=sem, vmem_limit_bytes=VMEM_LIMIT)


def _dot(a, b, precision=None):
    return jnp.dot(a, b, preferred_element_type=F32, precision=precision)


def _dot_nt(a, b, precision=None):
    return lax.dot_general(a, b, (((1,), (1,)), ((), ())), preferred_element_type=F32, precision=precision)


def _dot_tn(a, b, precision=None):
    return lax.dot_general(a, b, (((0,), (0,)), ((), ())), preferred_element_type=F32, precision=precision)


def _softplus2(z):
    return jnp.maximum(z, 0.0) + jnp.log(1.0 + jnp.exp2(-jnp.abs(z))) * LOG2E


def _proj_kernel(x_ref, g_ref, w_ref, ws_ref, qg_ref, kg_ref, p_ref,
                 q_ref, k_ref, v_ref, rest_ref, small_ref, h_ref, *, q_scale):
    j = pl.program_id(1)

    @pl.when(j == 0)
    def _():
        x = x_ref[...]
        ms = jnp.mean(x * x, axis=-1, keepdims=True)
        h = (x * lax.rsqrt(ms + EPS) * g_ref[...]).astype(BF16)
        h_ref[...] = h
        small_ref[...] = _dot(h, ws_ref[...])

    y = _dot(h_ref[...], w_ref[...])

    def head_norm(y, gain):
        sq = y * y
        hi = sq.astype(BF16)
        lo = (sq - hi.astype(F32)).astype(BF16)
        ms = _dot(hi, p_ref[...]) + _dot(lo, p_ref[...])
        return y * lax.rsqrt(ms + EPS) * gain

    @pl.when(j == 0)
    def _():
        q_ref[...] = (head_norm(y, qg_ref[...]) * q_scale).astype(BF16)

    @pl.when(j == 1)
    def _():
        k_ref[...] = head_norm(y, kg_ref[...])

    @pl.when(j == 2)
    def _():
        v_ref[...] = y

    @pl.when(j >= 3)
    def _():
        rest_ref[...] = y


def _proj(x2d, g, w_main, w_small, qg, kg, pmat, tm):
    n, d = x2d.shape
    ncol = w_main.shape[1] // COL_TILE
    nrest = w_main.shape[1] - 3 * COL_TILE
    kern = functools.partial(_proj_kernel, q_scale=SB_HEAD_DIM ** -0.5 * LOG2E)
    row = lambda i, j: (i, 0)
    const = lambda i, j: (0, 0)
    return pl.pallas_call(
        kern,
        grid=(n // tm, ncol),
        in_specs=[
            pl.BlockSpec((tm, d), row),
            pl.BlockSpec((1, d), const),
            pl.BlockSpec((d, COL_TILE), lambda i, j: (0, j)),
            pl.BlockSpec((d, LANES), const),
            pl.BlockSpec((1, COL_TILE), const),
            pl.BlockSpec((1, COL_TILE), const),
            pl.BlockSpec((COL_TILE, COL_TILE), const),
        ],
        out_specs=[
            pl.BlockSpec((tm, COL_TILE), row),
            pl.BlockSpec((tm, COL_TILE), row),
            pl.BlockSpec((tm, COL_TILE), row),
            pl.BlockSpec((tm, COL_TILE), lambda i, j: (i, jnp.maximum(j - 3, 0))),
            pl.BlockSpec((tm, LANES), row),
        ],
        out_shape=[
            jax.ShapeDtypeStruct((n, COL_TILE), BF16),
            jax.ShapeDtypeStruct((n, COL_TILE), F32),
            jax.ShapeDtypeStruct((n, COL_TILE), F32),
            jax.ShapeDtypeStruct((n, nrest), F32),
            jax.ShapeDtypeStruct((n, LANES), F32),
        ],
        scratch_shapes=[pltpu.VMEM((tm, d), BF16)],
        compiler_params=_cparams(("parallel", "arbitrary")),
        name="proj",
    )(x2d, g, w_main, w_small, qg, kg, pmat)


def _sb_prompt_kernel(q_ref, k_ref, v_ref, bias_ref, u_ref, o_ref,
                      qa_ref, z_ref, sp_ref, a_ref, m0_ref, acc_ref, r_ref):
    qi = pl.program_id(2)
    tq = q_ref.shape[1]
    tk = u_ref.shape[0]
    ratio = tq // tk
    n_tiles = ratio * (qi + 1)
    pair = (0, 1)

    lane = lax.broadcasted_iota(jnp.int32, (tq, LANES), 1)
    for e in pair:
        own = (lane < SB_HEAD_DIM) if e == 0 else (lane >= SB_HEAD_DIM)
        qa_ref[e] = jnp.where(own, q_ref[0], bias_ref[0, e:e + 1, :].astype(BF16))

    def key_start(n):
        return pl.multiple_of((n_tiles - 1 - n) * tk, tk)

    def scores(n, e, slot, diag):
        kt = k_ref[0, e, :, pl.ds(key_start(n), tk)]
        z = _dot(qa_ref[e], kt)
        sp = _softplus2(z)
        if diag:
            col = lax.broadcasted_iota(jnp.int32, (tq, tk), 1) + (tq - (n + 1) * tk)
            valid = col < lax.broadcasted_iota(jnp.int32, (tq, tk), 0)
            sp = jnp.where(valid, sp, 0.0)
            z = jnp.where(valid, z, MASKED_LOGIT)
        z_ref[e, slot] = z
        sp_ref[e, slot] = sp.astype(BF16)

    def weights(e, slot):
        m = _dot(sp_ref[e, slot], u_ref[...])
        a_ref[e, slot] = jnp.exp2(z_ref[e, slot] - m).astype(BF16)
        m0_ref[e, slot] = m[:, 0:1]

    def accumulate(n, e, slot):
        vt = v_ref[0, pl.ds(key_start(n), tk), :]
        pv = _dot(a_ref[e, slot], vt)
        r = r_ref[e]
        acc_ref[e] += jnp.exp2(r) * pv
        r_ref[e] = r - m0_ref[e, slot]

    acc_ref[...] = jnp.zeros_like(acc_ref)
    r_ref[...] = jnp.zeros_like(r_ref)
    for e in pair:
        scores(0, e, 0, True)
    for e in pair:
        scores(1, e, 1, True)
    for e in pair:
        weights(e, 0)

    def body(p, carry):
        s = 2 * p
        for e in pair:
            accumulate(s, e, 0)
        for e in pair:
            weights(e, 1)
        for e in pair:
            scores(s + 2, e, 0, False)
        for e in pair:
            accumulate(s + 1, e, 1)
        for e in pair:
            weights(e, 0)
        for e in pair:
            scores(s + 3, e, 1, False)
        return carry

    lax.fori_loop(0, qi, body, 0)
    for e in pair:
        accumulate(n_tiles - 2, e, 0)
    for e in pair:
        weights(e, 1)
    for e in pair:
        accumulate(n_tiles - 1, e, 1)
    o_ref[0] = jnp.where(lane < SB_HEAD_DIM, acc_ref[0], acc_ref[1]).astype(o_ref.dtype)


def _sb_prompt(q, k_aug, v, bias_rows, umat):
    nb, t, _ = q.shape
    npair = SB_HEADS // 2
    assert SB_TQ == 2 * SB_TK and 2 * SB_HEAD_DIM == LANES
    return pl.pallas_call(
        _sb_prompt_kernel,
        grid=(nb, npair, t // SB_TQ),
        in_specs=[
            pl.BlockSpec((1, SB_TQ, LANES), lambda b, p, i: (b, i, p)),
            pl.BlockSpec((1, 2, LANES, t), lambda b, p, i: (b * npair + p, 0, 0, 0)),
            pl.BlockSpec((1, t, LANES), lambda b, p, i: (b, 0, p)),
            pl.BlockSpec((1, 2, LANES), lambda b, p, i: (p, 0, 0)),
            pl.BlockSpec((SB_TK, SB_TK), lambda b, p, i: (0, 0)),
        ],
        out_specs=pl.BlockSpec((1, SB_TQ, LANES), lambda b, p, i: (b, i, p)),
        out_shape=jax.ShapeDtypeStruct((nb, t, SB_WIDTH), BF16),
        scratch_shapes=[
            pltpu.VMEM((2, SB_TQ, LANES), BF16),
            pltpu.VMEM((2, 2, SB_TQ, SB_TK), F32),
            pltpu.VMEM((2, 2, SB_TQ, SB_TK), BF16),
            pltpu.VMEM((2, 2, SB_TQ, SB_TK), BF16),
            pltpu.VMEM((2, 2, SB_TQ, 1), F32),
            pltpu.VMEM((2, SB_TQ, LANES), F32),
            pltpu.VMEM((2, SB_TQ, 1), F32),
        ],
        compiler_params=_cparams(("parallel", "parallel", "arbitrary")),
        name="sb_prompt",
    )(q, k_aug, v, bias_rows, umat)


def _sb_decode_kernel(pt_ref, q_ref, kn_ref, vn_ref, bias_ref, u_ref, *rest):
    del pt_ref
    npg = DEC_PAGES_PER_STEP
    kp_refs = rest[:npg]
    vp_refs = rest[npg:2 * npg]
    o_ref, acc_ref, r_ref = rest[2 * npg:]
    g = pl.program_id(1)
    nh, tq = SB_HEADS, DEC_TQ

    def head_rows(ref, h):
        return ref[0, pl.ds(h, PAGE_SIZE, stride=nh), :]

    def scores(k_ref, valid):
        z = jnp.concatenate([_dot_nt(q_ref[0, h * tq:(h + 1) * tq, :], head_rows(k_ref, h))
                             for h in range(nh)], axis=0) + bias_ref[...]
        sp = _softplus2(z)
        if valid is not None:
            sp = jnp.where(valid, sp, 0.0)
            z = jnp.where(valid, z, MASKED_LOGIT)
        m = _dot(sp.astype(BF16), u_ref[...])
        return z - m, m[:, 0:1]

    def attend(blocks):
        scored = [scores(k_ref, valid) for k_ref, _, valid in blocks]
        r = r_ref[...]
        out = [None] * nh
        for (e, m0), (_, v_ref, _) in zip(scored, blocks):
            a = jnp.exp2(e + r)
            for h in range(nh):
                pv = _dot(a[h * tq:(h + 1) * tq, :], head_rows(v_ref, h))
                out[h] = pv if out[h] is None else out[h] + pv
            r = r - m0
        r_ref[...] = r
        acc_ref[...] += jnp.concatenate(out, axis=0)

    pages = [(kp_refs[i], vp_refs[i], None) for i in reversed(range(npg))]

    @pl.when(g == 0)
    def _():
        acc_ref[...] = jnp.zeros_like(acc_ref)
        r_ref[...] = jnp.zeros_like(r_ref)
        key = lax.broadcasted_iota(jnp.int32, (nh * tq, PAGE_SIZE), 1)
        qry = lax.broadcasted_iota(jnp.int32, (nh * tq, PAGE_SIZE), 0) % tq
        attend([(kn_ref, vn_ref, key < qry)])

    attend(pages)

    @pl.when(g == pl.num_programs(1) - 1)
    def _():
        o_ref[0] = acc_ref[...].astype(o_ref.dtype)


def _sb_decode(page_ids, q_rows, k_new, v_new, bias_col, umat, cache_k3, cache_v3):
    nb, n_pages = page_ids.shape
    npg = DEC_PAGES_PER_STEP
    ngroups = n_pages // npg
    flat_rows = PAGE_SIZE * SB_HEADS
    nrow = SB_HEADS * DEC_TQ

    def page_spec(i):
        return pl.BlockSpec((1, flat_rows, SB_HEAD_DIM),
                            lambda b, g, pt: (pt[b, (ngroups - 1 - g) * npg + i], 0, 0))

    per_b = lambda b, g, pt: (b, 0, 0)
    const = lambda b, g, pt: (0, 0)
    grid_spec = pltpu.PrefetchScalarGridSpec(
        num_scalar_prefetch=1,
        grid=(nb, ngroups),
        in_specs=[
            pl.BlockSpec((1, nrow, SB_HEAD_DIM), per_b),
            pl.BlockSpec((1, flat_rows, SB_HEAD_DIM), per_b),
            pl.BlockSpec((1, flat_rows, SB_HEAD_DIM), per_b),
            pl.BlockSpec((nrow, 1), const),
            pl.BlockSpec((PAGE_SIZE, PAGE_SIZE), const),
        ] + [page_spec(i) for i in range(npg)] * 2,
        out_specs=pl.BlockSpec((1, nrow, SB_HEAD_DIM), per_b),
        scratch_shapes=[
            pltpu.VMEM((nrow, SB_HEAD_DIM), F32),
            pltpu.VMEM((nrow, 1), F32),
        ],
    )
    return pl.pallas_call(
        _sb_decode_kernel,
        grid_spec=grid_spec,
        out_shape=jax.ShapeDtypeStruct((nb, nrow, SB_HEAD_DIM), BF16),
        compiler_params=_cparams(("parallel", "arbitrary")),
        name="sb_decode",
    )(page_ids, q_rows, k_new, v_new, bias_col, umat, *([cache_k3] * npg), *([cache_v3] * npg))


def _convb_kernel(b_ref, c_ref, x_ref, cp_ref, xp_ref, buf_ref, w_ref, o_ref, tail_ref, u_ref):
    i = pl.program_id(1)
    tt = b_ref.shape[1]
    u = c_ref[0] * x_ref[0]
    prev = jnp.where(i == 0, buf_ref[0], cp_ref[0] * xp_ref[0])
    u_ref[0:SUBLANES, :] = prev
    u_ref[SUBLANES:, :] = u
    base = SUBLANES - (SC_TAPS - 1)
    y = w_ref[0:1, :] * u_ref[base:base + tt, :]
    for j in range(1, SC_TAPS):
        y = y + w_ref[j:j + 1, :] * u_ref[base + j:base + j + tt, :]
    o_ref[0] = (b_ref[0] * y).astype(o_ref.dtype)
    tail_ref[0] = u[tt - SUBLANES:, :]


def _convb(rest3, buf8, w8, tt):
    nb, t, _ = rest3.shape
    r8 = tt // SUBLANES
    c0 = REST_BCX // COL_TILE
    cur = lambda c: pl.BlockSpec((1, tt, COL_TILE), lambda b, i: (b, i, c))
    prev = lambda c: pl.BlockSpec((1, SUBLANES, COL_TILE), lambda b, i: (b, jnp.maximum(i * r8 - 1, 0), c))
    return pl.pallas_call(
        _convb_kernel,
        grid=(nb, t // tt),
        in_specs=[cur(c0), cur(c0 + 1), cur(c0 + 2), prev(c0 + 1), prev(c0 + 2),
                  pl.BlockSpec((1, SUBLANES, COL_TILE), lambda b, i: (b, 0, 0)),
                  pl.BlockSpec((SUBLANES, COL_TILE), lambda b, i: (0, 0))],
        out_specs=[pl.BlockSpec((1, tt, COL_TILE), lambda b, i: (b, i, 0)),
                   pl.BlockSpec((1, SUBLANES, COL_TILE), lambda b, i: (b, 0, 0))],
        out_shape=[jax.ShapeDtypeStruct((nb, t, COL_TILE), BF16),
                   jax.ShapeDtypeStruct((nb, SUBLANES, COL_TILE), F32)],
        scratch_shapes=[pltpu.VMEM((tt + SUBLANES, COL_TILE), F32)],
        compiler_params=_cparams(("parallel", "arbitrary")),
        name="convb",
    )(rest3, rest3, rest3, rest3, rest3, buf8, w8)


def _split_bf16(x):
    hi = x.astype(BF16)
    return hi, (x - hi.astype(F32)).astype(BF16)


def _dot3(a, b, nt=False):
    ah, al = _split_bf16(a)
    bh, bl = _split_bf16(b)
    d = _dot_nt if nt else _dot
    return d(ah, bh) + (d(al, bh) + d(ah, bl))


def _dotb(a, b):
    return _dot(a.astype(BF16), b.astype(BF16))


def _gdn_kernel(x_ref, z_ref, sm_ref, xp_ref, buf_ref, s0_ref, w_ref, nega_ref, dtb_ref, og_ref,
                o_ref, s_out_ref, s_ref, xb_ref, *, n_valid):
    i = pl.program_id(1)
    rows = x_ref.shape[1]
    c = GDN_CHUNK
    nh = GDN_HEADS
    hd = GDN_HEAD_DIM

    @pl.when(i == 0)
    def _():
        s_ref[...] = s0_ref[0]

    xb_ref[0:SUBLANES, :] = jnp.where(i == 0, buf_ref[0], xp_ref[0])
    xb_ref[SUBLANES:, :] = x_ref[0]
    base = SUBLANES - (GDN_TAPS - 1)
    y = w_ref[0:1, :] * xb_ref[base:base + rows, :]
    for j in range(1, GDN_TAPS):
        y = y + w_ref[j:j + 1, :] * xb_ref[base + j:base + j + rows, :]
    act = y * jax.nn.sigmoid(y)

    sm = sm_ref[0]
    beta_all = jax.nn.sigmoid(sm)
    pre = sm + dtb_ref[...]
    g_all = nega_ref[...] * (jnp.maximum(pre, 0.0) + jnp.log1p(jnp.exp(-jnp.abs(pre))))
    if n_valid < rows:
        live = lax.broadcasted_iota(jnp.int32, (rows, LANES), 0) < n_valid
        beta_all = jnp.where(live, beta_all, 0.0)
        g_all = jnp.where(live, g_all, 0.0)

    rr = lax.broadcasted_iota(jnp.int32, (rows, rows), 0)
    cc = lax.broadcasted_iota(jnp.int32, (rows, rows), 1)
    gcum = _dot3(((rr // c == cc // c) & (cc <= rr)).astype(F32), g_all)

    row = lax.broadcasted_iota(jnp.int32, (c, c), 0)
    col = lax.broadcasted_iota(jnp.int32, (c, c), 1)
    incl = col <= row
    strict = col < row
    lane = lax.broadcasted_iota(jnp.int32, (c, LANES), 1)
    wide = nh * c
    eye_w = (lax.broadcasted_iota(jnp.int32, (c, wide), 0)
             == lax.broadcasted_iota(jnp.int32, (c, wide), 1) % c).astype(F32)
    same_block = (lax.broadcasted_iota(jnp.int32, (wide, wide), 0) // c
                  == lax.broadcasted_iota(jnp.int32, (wide, wide), 1) // c)

    def times_blockdiag(x, p):
        xh, xl = _split_bf16(x)
        ph, pl_ = _split_bf16(p)
        zero = jnp.zeros((), BF16)
        bdh = jnp.where(same_block, jnp.tile(ph, (nh, 1)), zero)
        bdl = jnp.where(same_block, jnp.tile(pl_, (nh, 1)), zero)
        return _dot(xh, bdh) + (_dot(xl, bdh) + _dot(xh, bdl))

    nchunks = rows // c
    chunks = []
    for ci in range(nchunks):
        r0 = ci * c
        heads = []
        for h in range(nh):
            qh = act[r0:r0 + c, h * hd:(h + 1) * hd]
            kh = act[r0:r0 + c, GDN_WIDTH + h * hd:GDN_WIDTH + (h + 1) * hd]
            vh = act[r0:r0 + c, 2 * GDN_WIDTH + h * hd:2 * GDN_WIDTH + (h + 1) * hd]
            qh = qh * lax.rsqrt(jnp.sum(qh * qh, axis=-1, keepdims=True) + EPS) * (hd ** -0.5)
            kh = kh * lax.rsqrt(jnp.sum(kh * kh, axis=-1, keepdims=True) + EPS)
            b = beta_all[r0:r0 + c, h:h + 1]
            gc = gcum[r0:r0 + c, nh + h:nh + h + 1]
            xl = jnp.where(lane == 0, gc, jnp.where(lane == 1, 1.0, 0.0))
            yl = jnp.where(lane == 0, 1.0, jnp.where(lane == 1, -gc, 0.0))
            d = _dot3(xl, yl, nt=True)
            decay = jnp.where(incl, jnp.exp(jnp.where(incl, d, 0.0)), 0.0)
            a = jnp.where(strict, b * decay * _dot3(kh, kh, nt=True), 0.0)
            heads.append((qh, kh, vh, b, gc, decay, a))
        chunks.append(heads)

    ps = [jnp.concatenate([hh[6] for hh in heads], axis=1) for heads in chunks]
    tinvs = [eye_w - p for p in ps]
    for _ in range(int(math.log2(c)) - 1):
        ps = [times_blockdiag(p, p) for p in ps]
        tinvs = [t + times_blockdiag(t, p) for t, p in zip(tinvs, ps)]

    prepared = []
    for heads, tinv in zip(chunks, tinvs):
        per_head = []
        for h, (qh, kh, vh, b, gc, decay, _) in enumerate(heads):
            th = tinv[:, h * c:(h + 1) * c]
            eg = jnp.exp(gc)
            g_last = gc[c - 1:c, :]
            u = _dotb(th, b * vh)
            wk = _dotb(th, (b * eg) * kh).astype(BF16)
            qk = jnp.where(incl, _dot_nt(qh.astype(BF16), kh.astype(BF16)) * decay, 0.0).astype(BF16)
            per_head.append((u, wk, qk, (eg * qh).astype(BF16),
                             (kh * jnp.exp(g_last - gc)).astype(BF16), jnp.exp(g_last)))
        prepared.append(per_head)

    state = [s_ref[h] for h in range(nh)]
    for ci, per_head in enumerate(prepared):
        r0 = ci * c
        for h, (u, wk, qk, egq, kd, dec_last) in enumerate(per_head):
            s = state[h]
            sb = s.astype(BF16)
            w = u - _dot(wk, sb)
            wb = w.astype(BF16)
            o = _dot(egq, sb) + _dot(qk, wb)
            state[h] = dec_last * s + _dot_tn(kd, wb)
            on = o * lax.rsqrt(jnp.mean(o * o, axis=-1, keepdims=True) + EPS) * og_ref[...]
            zh = z_ref[0, r0:r0 + c, h * hd:(h + 1) * hd]
            o_ref[0, r0:r0 + c, h * hd:(h + 1) * hd] = (on * (zh * jax.nn.sigmoid(zh))).astype(o_ref.dtype)
    for h in range(nh):
        s_ref[h] = state[h]

    @pl.when(i == pl.num_programs(1) - 1)
    def _():
        s_out_ref[0] = s_ref[...]


def _gdn(rest3, small3, buf8, s0, w8, nega_row, dtb_row, og_row, c, n_valid):
    nb, t, _ = rest3.shape
    r8 = c // SUBLANES
    qkv_w = 3 * GDN_WIDTH
    kern = functools.partial(_gdn_kernel, n_valid=n_valid)
    const2 = lambda b, i: (0, 0)
    return pl.pallas_call(
        kern,
        grid=(nb, t // c),
        in_specs=[
            pl.BlockSpec((1, c, qkv_w), lambda b, i: (b, i, REST_QKV // qkv_w)),
            pl.BlockSpec((1, c, GDN_WIDTH), lambda b, i: (b, i, REST_Z // GDN_WIDTH)),
            pl.BlockSpec((1, c, LANES), lambda b, i: (b, i, 0)),
            pl.BlockSpec((1, SUBLANES, qkv_w), lambda b, i: (b, jnp.maximum(i * r8 - 1, 0), REST_QKV // qkv_w)),
            pl.BlockSpec((1, SUBLANES, qkv_w), lambda b, i: (b, 0, 0)),
            pl.BlockSpec((1, GDN_HEADS, GDN_HEAD_DIM, GDN_HEAD_DIM), lambda b, i: (b, 0, 0, 0)),
            pl.BlockSpec((SUBLANES, qkv_w), const2),
            pl.BlockSpec((1, LANES), const2),
            pl.BlockSpec((1, LANES), const2),
            pl.BlockSpec((1, GDN_HEAD_DIM), const2),
        ],
        out_specs=[
            pl.BlockSpec((1, c, GDN_WIDTH), lambda b, i: (b, i, 0)),
            pl.BlockSpec((1, GDN_HEADS, GDN_HEAD_DIM, GDN_HEAD_DIM), lambda b, i: (b, 0, 0, 0)),
        ],
        out_shape=[
            jax.ShapeDtypeStruct((nb, t, GDN_WIDTH), BF16),
            jax.ShapeDtypeStruct((nb, GDN_HEADS, GDN_HEAD_DIM, GDN_HEAD_DIM), F32),
        ],
        scratch_shapes=[
            pltpu.VMEM((GDN_HEADS, GDN_HEAD_DIM, GDN_HEAD_DIM), F32),
            pltpu.VMEM((c + SUBLANES, qkv_w), F32),
        ],
        compiler_params=_cparams(("parallel", "arbitrary")),
        name="gdn",
    )(rest3, rest3, small3, rest3, buf8, s0, w8, nega_row, dtb_row, og_row)


def _merge_kernel(x_ref, a_ref, b_ref, c_ref, gl_ref, wpa_ref, wpb_ref, wpc_ref, wo_ref, o_ref):
    d = x_ref.shape[1]
    gl = gl_ref[...]
    mix = jax.nn.sigmoid(gl[:, 0:d]) * _dot(a_ref[...], wpa_ref[...])
    mix = mix + jax.nn.sigmoid(gl[:, d:2 * d]) * _dot(b_ref[...], wpb_ref[...])
    mix = mix + jax.nn.sigmoid(gl[:, 2 * d:3 * d]) * _dot(c_ref[...], wpc_ref[...])
    o_ref[...] = x_ref[...] + _dot(mix.astype(BF16), wo_ref[...])


def _merge(x2d, out_a, out_b, out_c, rest, wpa, wpb, wpc, wo, tm):
    n, d = x2d.shape
    row = lambda i: (i, 0)
    const = lambda i: (0, 0)
    return pl.pallas_call(
        _merge_kernel,
        grid=(n // tm,),
        in_specs=[
            pl.BlockSpec((tm, d), row),
            pl.BlockSpec((tm, COL_TILE), row),
            pl.BlockSpec((tm, COL_TILE), row),
            pl.BlockSpec((tm, COL_TILE), row),
            pl.BlockSpec((tm, N_BRANCH * d), row),
            pl.BlockSpec((COL_TILE, d), const),
            pl.BlockSpec((COL_TILE, d), const),
            pl.BlockSpec((COL_TILE, d), const),
            pl.BlockSpec((d, d), const),
        ],
        out_specs=pl.BlockSpec((tm, d), row),
        out_shape=jax.ShapeDtypeStruct((n, d), F32),
        compiler_params=_cparams(("parallel",)),
        name="merge",
    )(x2d, out_a, out_b, out_c, rest, wpa, wpb, wpc, wo)


def _mlp_kernel(x_ref, g_ref, wu_ref, wd_ref, o_ref, h_ref, acc_ref):
    j = pl.program_id(1)

    @pl.when(j == 0)
    def _():
        x = x_ref[...]
        ms = jnp.mean(x * x, axis=-1, keepdims=True)
        h_ref[...] = (x * lax.rsqrt(ms + EPS) * g_ref[...]).astype(BF16)
        acc_ref[...] = x

    u = jnp.maximum(_dot(h_ref[...], wu_ref[...]), 0.0)
    acc_ref[...] += _dot((u * u).astype(BF16), wd_ref[...])

    @pl.when(j == pl.num_programs(1) - 1)
    def _():
        o_ref[...] = acc_ref[...]


def _mlp(x2d, g, w_up, w_down, tm, tf):
    n, d = x2d.shape
    dff = w_up.shape[1]
    return pl.pallas_call(
        _mlp_kernel,
        grid=(n // tm, dff // tf),
        in_specs=[
            pl.BlockSpec((tm, d), lambda i, j: (i, 0)),
            pl.BlockSpec((1, d), lambda i, j: (0, 0)),
            pl.BlockSpec((d, tf), lambda i, j: (0, j)),
            pl.BlockSpec((tf, d), lambda i, j: (j, 0)),
        ],
        out_specs=pl.BlockSpec((tm, d), lambda i, j: (i, 0)),
        out_shape=jax.ShapeDtypeStruct((n, d), F32),
        scratch_shapes=[pltpu.VMEM((tm, d), BF16), pltpu.VMEM((tm, d), F32)],
        compiler_params=_cparams(("parallel", "arbitrary")),
        name="mlp",
    )(x2d, g, w_up, w_down)


def _pad_rows_front(a, rows):
    pad = rows - a.shape[1]
    return jnp.pad(a, ((0, 0), (pad, 0), (0, 0)))


def _split3_bf16(v):
    hi = v.astype(BF16)
    mid = (v - hi.astype(F32)).astype(BF16)
    lo = (v - hi.astype(F32) - mid.astype(F32)).astype(BF16)
    return hi, mid, lo


def _layer_weights(l, norm1_g, w_in, qn_g, kn_g, sb_bias, w_conv_b, w_conv_c, a_log, dt_bias, onorm_g,
                   w_pa, w_pb, w_pc, w_o, norm2_g, w_up, w_down):
    d = w_in.shape[1]
    main_w = 3 * SB_WIDTH + 3 * SC_WIDTH + 4 * GDN_WIDTH
    small_w = 2 * GDN_HEADS
    wl = w_in[l]
    w_main = jnp.concatenate([wl[:, :3 * SB_WIDTH], wl[:, main_w + small_w:], wl[:, 3 * SB_WIDTH:main_w]],
                             axis=1).astype(BF16)
    w_small = jnp.pad(wl[:, main_w:main_w + small_w], ((0, 0), (0, LANES - small_w))).astype(BF16)
    head_of = jnp.arange(SB_WIDTH) // SB_HEAD_DIM
    pmat = (head_of[:, None] == head_of[None, :]).astype(F32) / SB_HEAD_DIM
    bias2 = sb_bias[l].astype(F32) * LOG2E
    nega = -jnp.exp(a_log[l].astype(F32))
    zeros4 = jnp.zeros((GDN_HEADS,), F32)
    pad_lanes = lambda v: jnp.pad(v, (0, LANES - v.shape[0]))[None, :]
    return dict(
        g1=norm1_g[l][None, :], w_main=w_main, w_small=w_small,
        qg=jnp.tile(qn_g[l], SB_HEADS)[None, :], kg=jnp.tile(kn_g[l], SB_HEADS)[None, :],
        pmat=pmat.astype(BF16), bias2=bias2,
        wcb=jnp.pad(w_conv_b[l], ((0, SUBLANES - SC_TAPS), (0, 0))),
        wcc=jnp.pad(w_conv_c[l], ((0, SUBLANES - GDN_TAPS), (0, 0))),
        nega_row=pad_lanes(jnp.concatenate([zeros4, nega])),
        dtb_row=pad_lanes(jnp.concatenate([zeros4, dt_bias[l].astype(F32)])),
        og_row=onorm_g[l][None, :],
        wpa=w_pa[l].astype(BF16), wpb=w_pb[l].astype(BF16), wpc=w_pc[l].astype(BF16),
        wo=w_o[l].astype(BF16), g2=norm2_g[l][None, :],
        w_up=w_up[l].astype(BF16), w_down=w_down[l].astype(BF16),
    )


def _prompt_attention(q, k, v, bias2):
    nb, t, _ = q.shape
    npair = SB_HEADS // 2
    hi, mid, lo = _split3_bf16(bias2)
    parts = jnp.stack([hi, mid, lo], axis=-1).astype(F32).reshape(npair, 2, 3)
    parts = jnp.pad(parts, ((0, 0), (0, 0), (0, SB_HEAD_DIM - 3)))
    zeros = jnp.zeros_like(parts[:, 0])
    bias_rows = jnp.stack([jnp.concatenate([zeros, parts[:, 0]], axis=-1),
                           jnp.concatenate([parts[:, 1], zeros], axis=-1)], axis=1)
    k_t = k.astype(BF16).reshape(nb, t, npair, 2, SB_HEAD_DIM).transpose(0, 2, 3, 4, 1)
    ones_rows = jnp.concatenate([jnp.ones((nb, npair, 3, t), BF16),
                                 jnp.zeros((nb, npair, SB_HEAD_DIM - 3, t), BF16)], axis=2)
    k_aug = jnp.stack([jnp.concatenate([k_t[:, :, 0], ones_rows], axis=2),
                       jnp.concatenate([ones_rows, k_t[:, :, 1]], axis=2)], axis=2)
    idx = jnp.arange(SB_TK)
    umat = (idx[:, None] >= idx[None, :]).astype(BF16)
    o = _sb_prompt(q, k_aug.reshape(nb * npair, 2, LANES, t), v.astype(BF16), bias_rows, umat)
    return o.reshape(nb * t, SB_WIDTH)


def _decode_attention(q, k, v, bias2, page_ids, cache_k2, cache_v2):
    nb, tq, _ = q.shape
    q_rows = q.astype(F32).reshape(nb, tq, SB_HEADS, SB_HEAD_DIM).transpose(0, 2, 1, 3)
    q_rows = q_rows.reshape(nb, SB_HEADS * tq, SB_HEAD_DIM)
    bias_col = jnp.repeat(bias2, tq)[:, None]
    idx = jnp.arange(PAGE_SIZE)
    umat = (idx[:, None] >= idx[None, :]).astype(BF16)
    as_page = lambda a: jnp.pad(a, ((0, 0), (0, PAGE_SIZE - tq), (0, 0))).reshape(
        nb, PAGE_SIZE * SB_HEADS, SB_HEAD_DIM)
    o = _sb_decode(page_ids, q_rows, as_page(k), as_page(v), bias_col, umat, cache_k2, cache_v2)
    o = o.reshape(nb, SB_HEADS, tq, SB_HEAD_DIM).transpose(0, 2, 1, 3)
    return o.reshape(nb * tq, SB_WIDTH)


def _trunk_layer(x3, w, attend, buf_b, buf_c, s0, chunk, n_valid):
    nb, t, d = x3.shape
    n = nb * t
    tm = min(ROW_TILE, n)
    q, k, v, rest, small = _proj(x3.reshape(n, d), w["g1"], w["w_main"], w["w_small"],
                                 w["qg"], w["kg"], w["pmat"], min(PROJ_ROW_TILE, n))
    out_a = attend(q.reshape(nb, t, SB_WIDTH), k.reshape(nb, t, SB_WIDTH), v.reshape(nb, t, SB_WIDTH))
    rest3 = rest.reshape(nb, t, rest.shape[1])
    out_b, tail_b = _convb(rest3, _pad_rows_front(buf_b, SUBLANES), w["wcb"], min(512, t))
    tg = -(-t // GDN_CHUNK) * GDN_CHUNK
    pad_t = lambda a: jnp.pad(a, ((0, 0), (0, tg - t), (0, 0)))
    out_c, s_new = _gdn(pad_t(rest3), pad_t(small.reshape(nb, t, LANES)), _pad_rows_front(buf_c, SUBLANES), s0,
                        w["wcc"], w["nega_row"], w["dtb_row"], w["og_row"], min(chunk, tg), n_valid)
    out_c = out_c[:, :t]
    x1 = _merge(x3.reshape(n, d), out_a, out_b.reshape(n, SC_WIDTH), out_c.reshape(n, GDN_WIDTH), rest,
                w["wpa"], w["wpb"], w["wpc"], w["wo"], tm)
    x2 = _mlp(x1, w["g2"], w["w_up"], w["w_down"], tm, 1024)
    nv = n_valid - (t - SUBLANES)
    new_b = tail_b[:, nv - (SC_TAPS - 1):nv]
    new_c = rest3[:, n_valid - (GDN_TAPS - 1):n_valid, REST_QKV:REST_QKV + 3 * GDN_WIDTH]
    return x2.reshape(nb, t, d), k, v, new_b, new_c, s_new


def kernel(x_prompt, x_sample, cache_k, cache_v, page_table, state_conv_b, state_conv_c, state_gdn,
           norm1_g, w_in, qn_g, kn_g, sb_bias, w_conv_b, w_conv_c, a_log, dt_bias, onorm_g,
           w_pa, w_pb, w_pc, w_o, norm2_g, w_up, w_down):
    depth = w_in.shape[0]
    nbp, tp, d = x_prompt.shape
    nbs, ts, _ = x_sample.shape
    n_phys = cache_k.shape[1]
    assert ts <= DEC_TQ and ts >= GDN_TAPS - 1 and tp % SB_TQ == 0
    assert page_table.shape[1] % DEC_PAGES_PER_STEP == 0 and cache_k.shape[2] == PAGE_SIZE

    xp = x_prompt
    xs = jnp.pad(x_sample, ((0, 0), (0, DEC_TQ - ts), (0, 0)))
    cache_k2 = cache_k.reshape(depth * n_phys, PAGE_SIZE * SB_HEADS, SB_HEAD_DIM)
    cache_v2 = cache_v.reshape(depth * n_phys, PAGE_SIZE * SB_HEADS, SB_HEAD_DIM)

    outs_p, outs_s = [], []
    for l in range(depth):
        w = _layer_weights(l, norm1_g, w_in, qn_g, kn_g, sb_bias, w_conv_b, w_conv_c, a_log, dt_bias,
                           onorm_g, w_pa, w_pb, w_pc, w_o, norm2_g, w_up, w_down)
        zb = jnp.zeros((nbp, SC_TAPS - 1, SC_WIDTH), F32)
        zc = jnp.zeros((nbp, GDN_TAPS - 1, 3 * GDN_WIDTH), F32)
        zs = jnp.zeros((nbp, GDN_HEADS, GDN_HEAD_DIM, GDN_HEAD_DIM), F32)
        attend_p = functools.partial(_prompt_attention, bias2=w["bias2"])
        xp, k_p, v_p, cb_p, cc_p, s_p = _trunk_layer(xp, w, attend_p, zb, zc, zs, GDN_ROWS_PER_STEP, tp)
        page_ids = page_table.astype(jnp.int32) + l * n_phys
        attend_s = functools.partial(_decode_attention, bias2=w["bias2"], page_ids=page_ids,
                                     cache_k2=cache_k2, cache_v2=cache_v2)
        xs, k_s, v_s, cb_s, cc_s, s_s = _trunk_layer(xs, w, attend_s, state_conv_b[l], state_conv_c[l],
                                                     state_gdn[l], GDN_ROWS_PER_STEP, ts)
        outs_p.append((k_p.reshape(nbp, tp, SB_HEADS, SB_HEAD_DIM), v_p.reshape(nbp, tp, SB_HEADS, SB_HEAD_DIM),
                       cb_p, cc_p, s_p))
        outs_s.append((k_s.reshape(nbs, DEC_TQ, SB_HEADS, SB_HEAD_DIM)[:, :ts],
                       v_s.reshape(nbs, DEC_TQ, SB_HEADS, SB_HEAD_DIM)[:, :ts], cb_s, cc_s, s_s))

    stack = lambda outs, i: jnp.stack([o[i] for o in outs])
    return (xp, xs[:, :ts],
            stack(outs_p, 0), stack(outs_p, 1), stack(outs_p, 2), stack(outs_p, 3), stack(outs_p, 4),
            stack(outs_s, 0), stack(outs_s, 1), stack(outs_s, 2), stack(outs_s, 3), stack(outs_s, 4))
```

```python
import functools
import math

import jax
import jax.numpy as jnp
from jax import lax
from jax.experimental import pallas as pl
from jax.experimental.pallas import tpu as pltpu

F32 = jnp.float32
BF16 = jnp.bfloat16
HIGHEST = lax.Precision.HIGHEST

EPS = 1e-6
LOG2E = 1.4426950408889634

SB_HEADS = 8
SB_HEAD_DIM = 64
SB_WIDTH = SB_HEADS * SB_HEAD_DIM
SC_WIDTH = 512
SC_TAPS = 3
GDN_HEADS = 4
GDN_HEAD_DIM = 128
GDN_WIDTH = GDN_HEADS * GDN_HEAD_DIM
GDN_TAPS = 4
GDN_CHUNK = 64
GDN_ROWS_PER_STEP = 4 * GDN_CHUNK
PAGE_SIZE = 128
N_BRANCH = 3
D_MODEL = 1024

REST_GATES = 0
REST_BCX = REST_GATES + N_BRANCH * D_MODEL
REST_QKV = REST_BCX + 3 * SC_WIDTH
REST_Z = REST_QKV + 3 * GDN_WIDTH

LANES = 128
SUBLANES = 8
COL_TILE = 512
ROW_TILE = 512
PROJ_ROW_TILE = 1024
VMEM_LIMIT = 48 * 1024 * 1024

SB_TQ = 512
SB_TK = 256
MASKED_LOGIT = -1e30
DEC_TQ = 8
DEC_PAGES_PER_STEP = 8


def _cparams(sem):
    return pltpu.CompilerParams(dimension_semantics=sem, vmem_limit_bytes=VMEM_LIMIT)


def _dot(a, b, precision=None):
    return jnp.dot(a, b, preferred_element_type=F32, precision=precision)


def _dot_nt(a, b, precision=None):
    return lax.dot_general(a, b, (((1,), (1,)), ((), ())), preferred_element_type=F32, precision=precision)


def _dot_tn(a, b, precision=None):
    return lax.dot_general(a, b, (((0,), (0,)), ((), ())), preferred_element_type=F32, precision=precision)


def _softplus2(z):
    return jnp.maximum(z, 0.0) + jnp.log(1.0 + jnp.exp2(-jnp.abs(z))) * LOG2E


def _proj_kernel(x_ref, g_ref, w_ref, ws_ref, qg_ref, kg_ref, p_ref,
                 q_ref, k_ref, v_ref, rest_ref, small_ref, h_ref, *, q_scale):
    j = pl.program_id(1)

    @pl.when(j == 0)
    def _():
        x = x_ref[...]
        ms = jnp.mean(x * x, axis=-1, keepdims=True)
        h = (x * lax.rsqrt(ms + EPS) * g_ref[...]).astype(BF16)
        h_ref[...] = h
        small_ref[...] = _dot(h, ws_ref[...])

    y = _dot(h_ref[...], w_ref[...])

    def head_norm(y, gain):
        sq = y * y
        hi = sq.astype(BF16)
        lo = (sq - hi.astype(F32)).astype(BF16)
        ms = _dot(hi, p_ref[...]) + _dot(lo, p_ref[...])
        return y * lax.rsqrt(ms + EPS) * gain

    @pl.when(j == 0)
    def _():
        q_ref[...] = (head_norm(y, qg_ref[...]) * q_scale).astype(BF16)

    @pl.when(j == 1)
    def _():
        k_ref[...] = head_norm(y, kg_ref[...])

    @pl.when(j == 2)
    def _():
        v_ref[...] = y

    @pl.when(j >= 3)
    def _():
        rest_ref[...] = y


def _proj(x2d, g, w_main, w_small, qg, kg, pmat, tm):
    n, d = x2d.shape
    ncol = w_main.shape[1] // COL_TILE
    nrest = w_main.shape[1] - 3 * COL_TILE
    kern = functools.partial(_proj_kernel, q_scale=SB_HEAD_DIM ** -0.5 * LOG2E)
    row = lambda i, j: (i, 0)
    const = lambda i, j: (0, 0)
    return pl.pallas_call(
        kern,
        grid=(n // tm, ncol),
        in_specs=[
            pl.BlockSpec((tm, d), row),
            pl.BlockSpec((1, d), const),
            pl.BlockSpec((d, COL_TILE), lambda i, j: (0, j)),
            pl.BlockSpec((d, LANES), const),
            pl.BlockSpec((1, COL_TILE), const),
            pl.BlockSpec((1, COL_TILE), const),
            pl.BlockSpec((COL_TILE, COL_TILE), const),
        ],
        out_specs=[
            pl.BlockSpec((tm, COL_TILE), row),
            pl.BlockSpec((tm, COL_TILE), row),
            pl.BlockSpec((tm, COL_TILE), row),
            pl.BlockSpec((tm, COL_TILE), lambda i, j: (i, jnp.maximum(j - 3, 0))),
            pl.BlockSpec((tm, LANES), row),
        ],
        out_shape=[
            jax.ShapeDtypeStruct((n, COL_TILE), BF16),
            jax.ShapeDtypeStruct((n, COL_TILE), F32),
            jax.ShapeDtypeStruct((n, COL_TILE), F32),
            jax.ShapeDtypeStruct((n, nrest), F32),
            jax.ShapeDtypeStruct((n, LANES), F32),
        ],
        scratch_shapes=[pltpu.VMEM((tm, d), BF16)],
        compiler_params=_cparams(("parallel", "arbitrary")),
        name="proj",
    )(x2d, g, w_main, w_small, qg, kg, pmat)


def _sb_prompt_kernel(q_ref, k_ref, v_ref, bias_ref, u_ref, o_ref,
                      qa_ref, z_ref, sp_ref, a_ref, m0_ref, acc_ref, r_ref):
    qi = pl.program_id(2)
    tq = q_ref.shape[1]
    tk = u_ref.shape[0]
    ratio = tq // tk
    n_tiles = ratio * (qi + 1)
    pair = (0, 1)

    lane = lax.broadcasted_iota(jnp.int32, (tq, LANES), 1)
    for e in pair:
        own = (lane < SB_HEAD_DIM) if e == 0 else (lane >= SB_HEAD_DIM)
        qa_ref[e] = jnp.where(own, q_ref[0], bias_ref[0, e:e + 1, :].astype(BF16))

    def key_start(n):
        return pl.multiple_of((n_tiles - 1 - n) * tk, tk)

    def scores(n, e, slot, diag):
        kt = k_ref[0, e, :, pl.ds(key_start(n), tk)]
        z = _dot(qa_ref[e], kt)
        sp = _softplus2(z)
        if diag:
            col = lax.broadcasted_iota(jnp.int32, (tq, tk), 1) + (tq - (n + 1) * tk)
            valid = col < lax.broadcasted_iota(jnp.int32, (tq, tk), 0)
            sp = jnp.where(valid, sp, 0.0)
            z = jnp.where(valid, z, MASKED_LOGIT)
        z_ref[e, slot] = z
        sp_ref[e, slot] = sp.astype(BF16)

    def weights(e, slot):
        m = _dot(sp_ref[e, slot], u_ref[...])
        a_ref[e, slot] = jnp.exp2(z_ref[e, slot] - m).astype(BF16)
        m0_ref[e, slot] = m[:, 0:1]

    def accumulate(n, e, slot):
        vt = v_ref[0, pl.ds(key_start(n), tk), :]
        pv = _dot(a_ref[e, slot], vt)
        r = r_ref[e]
        acc_ref[e] += jnp.exp2(r) * pv
        r_ref[e] = r - m0_ref[e, slot]

    acc_ref[...] = jnp.zeros_like(acc_ref)
    r_ref[...] = jnp.zeros_like(r_ref)
    for e in pair:
        scores(0, e, 0, True)
    for e in pair:
        scores(1, e, 1, True)
    for e in pair:
        weights(e, 0)

    def body(p, carry):
        s = 2 * p
        for e in pair:
            accumulate(s, e, 0)
        for e in pair:
            weights(e, 1)
        for e in pair:
            scores(s + 2, e, 0, False)
        for e in pair:
            accumulate(s + 1, e, 1)
        for e in pair:
            weights(e, 0)
        for e in pair:
            scores(s + 3, e, 1, False)
        return carry

    lax.fori_loop(0, qi, body, 0)
    for e in pair:
        accumulate(n_tiles - 2, e, 0)
    for e in pair:
        weights(e, 1)
    for e in pair:
        accumulate(n_tiles - 1, e, 1)
    o_ref[0] = jnp.where(lane < SB_HEAD_DIM, acc_ref[0], acc_ref[1]).astype(o_ref.dtype)


def _sb_prompt(q, k_aug, v, bias_rows, umat):
    nb, t, _ = q.shape
    npair = SB_HEADS // 2
    assert SB_TQ == 2 * SB_TK and 2 * SB_HEAD_DIM == LANES
    return pl.pallas_call(
        _sb_prompt_kernel,
        grid=(nb, npair, t // SB_TQ),
        in_specs=[
            pl.BlockSpec((1, SB_TQ, LANES), lambda b, p, i: (b, i, p)),
            pl.BlockSpec((1, 2, LANES, t), lambda b, p, i: (b * npair + p, 0, 0, 0)),
            pl.BlockSpec((1, t, LANES), lambda b, p, i: (b, 0, p)),
            pl.BlockSpec((1, 2, LANES), lambda b, p, i: (p, 0, 0)),
            pl.BlockSpec((SB_TK, SB_TK), lambda b, p, i: (0, 0)),
        ],
        out_specs=pl.BlockSpec((1, SB_TQ, LANES), lambda b, p, i: (b, i, p)),
        out_shape=jax.ShapeDtypeStruct((nb, t, SB_WIDTH), BF16),
        scratch_shapes=[
            pltpu.VMEM((2, SB_TQ, LANES), BF16),
            pltpu.VMEM((2, 2, SB_TQ, SB_TK), F32),
            pltpu.VMEM((2, 2, SB_TQ, SB_TK), BF16),
            pltpu.VMEM((2, 2, SB_TQ, SB_TK), BF16),
            pltpu.VMEM((2, 2, SB_TQ, 1), F32),
            pltpu.VMEM((2, SB_TQ, LANES), F32),
            pltpu.VMEM((2, SB_TQ, 1), F32),
        ],
        compiler_params=_cparams(("parallel", "parallel", "arbitrary")),
        name="sb_prompt",
    )(q, k_aug, v, bias_rows, umat)


def _sb_decode_kernel(pt_ref, q_ref, kn_ref, vn_ref, bias_ref, u_ref, *rest):
    del pt_ref
    npg = DEC_PAGES_PER_STEP
    kp_refs = rest[:npg]
    vp_refs = rest[npg:2 * npg]
    o_ref, acc_ref, r_ref = rest[2 * npg:]
    g = pl.program_id(1)
    nh, tq = SB_HEADS, DEC_TQ

    def side_by_side(refs):
        return refs[0][0] if len(refs) == 1 else jnp.concatenate([ref[0] for ref in refs], axis=1)

    def scores(k_refs, valid):
        z = _dot(q_ref[0], side_by_side(k_refs)) + bias_ref[...]
        sp = _softplus2(z)
        if valid is not None:
            sp = jnp.where(valid, sp, 0.0)
            z = jnp.where(valid, z, MASKED_LOGIT)
        nk = z.shape[1]
        m = _dot(sp.astype(BF16), u_ref[0:nk, 0:nk])
        return z - m, m[:, 0:1]

    def attend(blocks):
        scored = [scores(k_refs, valid) for k_refs, _, valid in blocks]
        r = r_ref[...]
        out = None
        for (e, m0), (_, v_refs, _) in zip(scored, blocks):
            a = jnp.exp2(e + r)
            pv = _dot_nt(a, side_by_side(v_refs))
            out = pv if out is None else out + pv
            r = r - m0
        r_ref[...] = r
        acc_ref[...] += out

    pages = [([kp_refs[i], kp_refs[i + 1]], [vp_refs[i], vp_refs[i + 1]], None)
             for i in reversed(range(0, npg, 2))]

    @pl.when(g == 0)
    def _():
        acc_ref[...] = jnp.zeros_like(acc_ref)
        r_ref[...] = jnp.zeros_like(r_ref)
        key = lax.broadcasted_iota(jnp.int32, (nh * tq, PAGE_SIZE), 1)
        qry = lax.broadcasted_iota(jnp.int32, (nh * tq, PAGE_SIZE), 0) % tq
        attend([([kn_ref], [vn_ref], key < qry)])

    attend(pages)

    @pl.when(g == pl.num_programs(1) - 1)
    def _():
        acc = acc_ref[...]
        rhead = lax.broadcasted_iota(jnp.int32, acc.shape, 0) // tq
        chead = lax.broadcasted_iota(jnp.int32, acc.shape, 1) // SB_HEAD_DIM
        picked = jnp.where(rhead == chead, acc, 0.0)
        folded = picked[:, 0:SB_HEAD_DIM]
        for h in range(1, nh):
            folded = folded + picked[:, h * SB_HEAD_DIM:(h + 1) * SB_HEAD_DIM]
        o_ref[0] = folded.astype(o_ref.dtype)


def _sb_decode(page_ids, q_rows, k_new, v_new, bias_col, umat, cache_k3, cache_v3):
    nb, n_pages = page_ids.shape
    npg = DEC_PAGES_PER_STEP
    ngroups = n_pages // npg
    nrow = SB_HEADS * DEC_TQ

    def page_spec(i):
        return pl.BlockSpec((1, SB_WIDTH, PAGE_SIZE),
                            lambda b, g, pt: (pt[b, (ngroups - 1 - g) * npg + i], 0, 0))

    per_b = lambda b, g, pt: (b, 0, 0)
    const = lambda b, g, pt: (0, 0)
    grid_spec = pltpu.PrefetchScalarGridSpec(
        num_scalar_prefetch=1,
        grid=(nb, ngroups),
        in_specs=[
            pl.BlockSpec((1, nrow, SB_WIDTH), per_b),
            pl.BlockSpec((1, SB_WIDTH, PAGE_SIZE), per_b),
            pl.BlockSpec((1, SB_WIDTH, PAGE_SIZE), per_b),
            pl.BlockSpec((nrow, 1), const),
            pl.BlockSpec((2 * PAGE_SIZE, 2 * PAGE_SIZE), const),
        ] + [page_spec(i) for i in range(npg)] * 2,
        out_specs=pl.BlockSpec((1, nrow, SB_HEAD_DIM), per_b),
        scratch_shapes=[
            pltpu.VMEM((nrow, SB_WIDTH), F32),
            pltpu.VMEM((nrow, 1), F32),
        ],
    )
    return pl.pallas_call(
        _sb_decode_kernel,
        grid_spec=grid_spec,
        out_shape=jax.ShapeDtypeStruct((nb, nrow, SB_HEAD_DIM), BF16),
        compiler_params=_cparams(("parallel", "arbitrary")),
        name="sb_decode",
    )(page_ids, q_rows, k_new, v_new, bias_col, umat, *([cache_k3] * npg), *([cache_v3] * npg))


def _convb_kernel(b_ref, c_ref, x_ref, cp_ref, xp_ref, buf_ref, w_ref, o_ref, tail_ref, u_ref):
    i = pl.program_id(1)
    tt = b_ref.shape[1]
    u = c_ref[0] * x_ref[0]
    prev = jnp.where(i == 0, buf_ref[0], cp_ref[0] * xp_ref[0])
    u_ref[0:SUBLANES, :] = prev
    u_ref[SUBLANES:, :] = u
    base = SUBLANES - (SC_TAPS - 1)
    y = w_ref[0:1, :] * u_ref[base:base + tt, :]
    for j in range(1, SC_TAPS):
        y = y + w_ref[j:j + 1, :] * u_ref[base + j:base + j + tt, :]
    o_ref[0] = (b_ref[0] * y).astype(o_ref.dtype)
    tail_ref[0] = u[tt - SUBLANES:, :]


def _convb(rest3, buf8, w8, tt):
    nb, t, _ = rest3.shape
    r8 = tt // SUBLANES
    c0 = REST_BCX // COL_TILE
    cur = lambda c: pl.BlockSpec((1, tt, COL_TILE), lambda b, i: (b, i, c))
    prev = lambda c: pl.BlockSpec((1, SUBLANES, COL_TILE), lambda b, i: (b, jnp.maximum(i * r8 - 1, 0), c))
    return pl.pallas_call(
        _convb_kernel,
        grid=(nb, t // tt),
        in_specs=[cur(c0), cur(c0 + 1), cur(c0 + 2), prev(c0 + 1), prev(c0 + 2),
                  pl.BlockSpec((1, SUBLANES, COL_TILE), lambda b, i: (b, 0, 0)),
                  pl.BlockSpec((SUBLANES, COL_TILE), lambda b, i: (0, 0))],
        out_specs=[pl.BlockSpec((1, tt, COL_TILE), lambda b, i: (b, i, 0)),
                   pl.BlockSpec((1, SUBLANES, COL_TILE), lambda b, i: (b, 0, 0))],
        out_shape=[jax.ShapeDtypeStruct((nb, t, COL_TILE), BF16),
                   jax.ShapeDtypeStruct((nb, SUBLANES, COL_TILE), F32)],
        scratch_shapes=[pltpu.VMEM((tt + SUBLANES, COL_TILE), F32)],
        compiler_params=_cparams(("parallel", "arbitrary")),
        name="convb",
    )(rest3, rest3, rest3, rest3, rest3, buf8, w8)


def _split_bf16(x):
    hi = x.astype(BF16)
    return hi, (x - hi.astype(F32)).astype(BF16)


def _dot3(a, b, nt=False):
    ah, al = _split_bf16(a)
    bh, bl = _split_bf16(b)
    d = _dot_nt if nt else _dot
    return d(ah, bh) + (d(al, bh) + d(ah, bl))


def _dotb(a, b):
    return _dot(a.astype(BF16), b.astype(BF16))


def _gdn_kernel(x_ref, z_ref, sm_ref, xp_ref, buf_ref, s0_ref, w_ref, nega_ref, dtb_ref, og_ref,
                o_ref, s_out_ref, s_ref, xb_ref, *, n_valid):
    i = pl.program_id(1)
    rows = x_ref.shape[1]
    c = GDN_CHUNK
    nh = GDN_HEADS
    hd = GDN_HEAD_DIM

    @pl.when(i == 0)
    def _():
        s_ref[...] = s0_ref[0]

    xb_ref[0:SUBLANES, :] = jnp.where(i == 0, buf_ref[0], xp_ref[0])
    xb_ref[SUBLANES:, :] = x_ref[0]
    base = SUBLANES - (GDN_TAPS - 1)
    y = w_ref[0:1, :] * xb_ref[base:base + rows, :]
    for j in range(1, GDN_TAPS):
        y = y + w_ref[j:j + 1, :] * xb_ref[base + j:base + j + rows, :]
    act = y * jax.nn.sigmoid(y)

    sm = sm_ref[0]
    beta_all = jax.nn.sigmoid(sm)
    pre = sm + dtb_ref[...]
    g_all = nega_ref[...] * (jnp.maximum(pre, 0.0) + jnp.log1p(jnp.exp(-jnp.abs(pre))))
    if n_valid < rows:
        live = lax.broadcasted_iota(jnp.int32, (rows, LANES), 0) < n_valid
        beta_all = jnp.where(live, beta_all, 0.0)
        g_all = jnp.where(live, g_all, 0.0)

    rr = lax.broadcasted_iota(jnp.int32, (rows, rows), 0)
    cc = lax.broadcasted_iota(jnp.int32, (rows, rows), 1)
    gcum = _dot3(((rr // c == cc // c) & (cc <= rr)).astype(F32), g_all)

    row = lax.broadcasted_iota(jnp.int32, (c, c), 0)
    col = lax.broadcasted_iota(jnp.int32, (c, c), 1)
    incl = col <= row
    strict = col < row
    wide = nh * c
    eye_w = (lax.broadcasted_iota(jnp.int32, (c, wide), 0)
             == lax.broadcasted_iota(jnp.int32, (c, wide), 1) % c).astype(F32)
    same_block = (lax.broadcasted_iota(jnp.int32, (wide, wide), 0) // c
                  == lax.broadcasted_iota(jnp.int32, (wide, wide), 1) // c)

    def times_blockdiag(x, p):
        xh, xl = _split_bf16(x)
        ph, pl_ = _split_bf16(p)
        zero = jnp.zeros((), BF16)
        bdh = jnp.where(same_block, jnp.tile(ph, (nh, 1)), zero)
        bdl = jnp.where(same_block, jnp.tile(pl_, (nh, 1)), zero)
        return _dot(xh, bdh) + (_dot(xl, bdh) + _dot(xh, bdl))

    nchunks = rows // c
    chunks = []
    for ci in range(nchunks):
        r0 = ci * c
        heads = []
        gcum_t = gcum[r0:r0 + c, :].T
        for h in range(nh):
            qh = act[r0:r0 + c, h * hd:(h + 1) * hd]
            kh = act[r0:r0 + c, GDN_WIDTH + h * hd:GDN_WIDTH + (h + 1) * hd]
            vh = act[r0:r0 + c, 2 * GDN_WIDTH + h * hd:2 * GDN_WIDTH + (h + 1) * hd]
            qh = qh * lax.rsqrt(jnp.sum(qh * qh, axis=-1, keepdims=True) + EPS) * (hd ** -0.5)
            kh = kh * lax.rsqrt(jnp.sum(kh * kh, axis=-1, keepdims=True) + EPS)
            b = beta_all[r0:r0 + c, h:h + 1]
            gc = gcum[r0:r0 + c, nh + h:nh + h + 1]
            d = gc - gcum_t[nh + h:nh + h + 1, :]
            decay = jnp.where(incl, jnp.exp(jnp.where(incl, d, 0.0)), 0.0)
            a = jnp.where(strict, b * decay * _dot3(kh, kh, nt=True), 0.0)
            heads.append((qh, kh, vh, b, gc, decay, a))
        chunks.append(heads)

    ps = [jnp.concatenate([hh[6] for hh in heads], axis=1) for heads in chunks]
    tinvs = [eye_w - p for p in ps]
    for _ in range(int(math.log2(c)) - 1):
        ps = [times_blockdiag(p, p) for p in ps]
        tinvs = [t + times_blockdiag(t, p) for t, p in zip(tinvs, ps)]

    prepared = []
    for heads, tinv in zip(chunks, tinvs):
        per_head = []
        for h, (qh, kh, vh, b, gc, decay, _) in enumerate(heads):
            th = tinv[:, h * c:(h + 1) * c]
            eg = jnp.exp(gc)
            g_last = gc[c - 1:c, :]
            u = _dotb(th, b * vh)
            wk = _dotb(th, (b * eg) * kh).astype(BF16)
            qk = jnp.where(incl, _dot_nt(qh.astype(BF16), kh.astype(BF16)) * decay, 0.0).astype(BF16)
            per_head.append((u, wk, qk, (eg * qh).astype(BF16),
                             (kh * jnp.exp(g_last - gc)).astype(BF16), jnp.exp(g_last)))
        prepared.append(per_head)

    state = [s_ref[h] for h in range(nh)]
    for ci, per_head in enumerate(prepared):
        r0 = ci * c
        for h, (u, wk, qk, egq, kd, dec_last) in enumerate(per_head):
            s = state[h]
            sb = s.astype(BF16)
            w = u - _dot(wk, sb)
            wb = w.astype(BF16)
            o = _dot(egq, sb) + _dot(qk, wb)
            state[h] = dec_last * s + _dot_tn(kd, wb)
            on = o * lax.rsqrt(jnp.mean(o * o, axis=-1, keepdims=True) + EPS) * og_ref[...]
            zh = z_ref[0, r0:r0 + c, h * hd:(h + 1) * hd]
            o_ref[0, r0:r0 + c, h * hd:(h + 1) * hd] = (on * (zh * jax.nn.sigmoid(zh))).astype(o_ref.dtype)
    for h in range(nh):
        s_ref[h] = state[h]

    @pl.when(i == pl.num_programs(1) - 1)
    def _():
        s_out_ref[0] = s_ref[...]


def _gdn(rest3, small3, buf8, s0, w8, nega_row, dtb_row, og_row, c, n_valid):
    nb, t, _ = rest3.shape
    r8 = c // SUBLANES
    qkv_w = 3 * GDN_WIDTH
    kern = functools.partial(_gdn_kernel, n_valid=n_valid)
    const2 = lambda b, i: (0, 0)
    return pl.pallas_call(
        kern,
        grid=(nb, t // c),
        in_specs=[
            pl.BlockSpec((1, c, qkv_w), lambda b, i: (b, i, REST_QKV // qkv_w)),
            pl.BlockSpec((1, c, GDN_WIDTH), lambda b, i: (b, i, REST_Z // GDN_WIDTH)),
            pl.BlockSpec((1, c, LANES), lambda b, i: (b, i, 0)),
            pl.BlockSpec((1, SUBLANES, qkv_w), lambda b, i: (b, jnp.maximum(i * r8 - 1, 0), REST_QKV // qkv_w)),
            pl.BlockSpec((1, SUBLANES, qkv_w), lambda b, i: (b, 0, 0)),
            pl.BlockSpec((1, GDN_HEADS, GDN_HEAD_DIM, GDN_HEAD_DIM), lambda b, i: (b, 0, 0, 0)),
            pl.BlockSpec((SUBLANES, qkv_w), const2),
            pl.BlockSpec((1, LANES), const2),
            pl.BlockSpec((1, LANES), const2),
            pl.BlockSpec((1, GDN_HEAD_DIM), const2),
        ],
        out_specs=[
            pl.BlockSpec((1, c, GDN_WIDTH), lambda b, i: (b, i, 0)),
            pl.BlockSpec((1, GDN_HEADS, GDN_HEAD_DIM, GDN_HEAD_DIM), lambda b, i: (b, 0, 0, 0)),
        ],
        out_shape=[
            jax.ShapeDtypeStruct((nb, t, GDN_WIDTH), BF16),
            jax.ShapeDtypeStruct((nb, GDN_HEADS, GDN_HEAD_DIM, GDN_HEAD_DIM), F32),
        ],
        scratch_shapes=[
            pltpu.VMEM((GDN_HEADS, GDN_HEAD_DIM, GDN_HEAD_DIM), F32),
            pltpu.VMEM((c + SUBLANES, qkv_w), F32),
        ],
        compiler_params=_cparams(("parallel", "arbitrary")),
        name="gdn",
    )(rest3, rest3, small3, rest3, buf8, s0, w8, nega_row, dtb_row, og_row)


def _merge_kernel(x_ref, a_ref, b_ref, c_ref, gl_ref, wpa_ref, wpb_ref, wpc_ref, wo_ref, o_ref):
    d = x_ref.shape[1]
    gl = gl_ref[...]
    mix = jax.nn.sigmoid(gl[:, 0:d]) * _dot(a_ref[...], wpa_ref[...])
    mix = mix + jax.nn.sigmoid(gl[:, d:2 * d]) * _dot(b_ref[...], wpb_ref[...])
    mix = mix + jax.nn.sigmoid(gl[:, 2 * d:3 * d]) * _dot(c_ref[...], wpc_ref[...])
    o_ref[...] = x_ref[...] + _dot(mix.astype(BF16), wo_ref[...])


def _merge(x2d, out_a, out_b, out_c, rest, wpa, wpb, wpc, wo, tm):
    n, d = x2d.shape
    row = lambda i: (i, 0)
    const = lambda i: (0, 0)
    return pl.pallas_call(
        _merge_kernel,
        grid=(n // tm,),
        in_specs=[
            pl.BlockSpec((tm, d), row),
            pl.BlockSpec((tm, COL_TILE), row),
            pl.BlockSpec((tm, COL_TILE), row),
            pl.BlockSpec((tm, COL_TILE), row),
            pl.BlockSpec((tm, N_BRANCH * d), row),
            pl.BlockSpec((COL_TILE, d), const),
            pl.BlockSpec((COL_TILE, d), const),
            pl.BlockSpec((COL_TILE, d), const),
            pl.BlockSpec((d, d), const),
        ],
        out_specs=pl.BlockSpec((tm, d), row),
        out_shape=jax.ShapeDtypeStruct((n, d), F32),
        compiler_params=_cparams(("parallel",)),
        name="merge",
    )(x2d, out_a, out_b, out_c, rest, wpa, wpb, wpc, wo)


def _mlp_kernel(x_ref, g_ref, wu_ref, wd_ref, o_ref, h_ref, acc_ref):
    j = pl.program_id(1)

    @pl.when(j == 0)
    def _():
        x = x_ref[...]
        ms = jnp.mean(x * x, axis=-1, keepdims=True)
        h_ref[...] = (x * lax.rsqrt(ms + EPS) * g_ref[...]).astype(BF16)
        acc_ref[...] = x

    u = jnp.maximum(_dot(h_ref[...], wu_ref[...]), 0.0)
    acc_ref[...] += _dot((u * u).astype(BF16), wd_ref[...])

    @pl.when(j == pl.num_programs(1) - 1)
    def _():
        o_ref[...] = acc_ref[...]


def _mlp(x2d, g, w_up, w_down, tm, tf):
    n, d = x2d.shape
    dff = w_up.shape[1]
    return pl.pallas_call(
        _mlp_kernel,
        grid=(n // tm, dff // tf),
        in_specs=[
            pl.BlockSpec((tm, d), lambda i, j: (i, 0)),
            pl.BlockSpec((1, d), lambda i, j: (0, 0)),
            pl.BlockSpec((d, tf), lambda i, j: (0, j)),
            pl.BlockSpec((tf, d), lambda i, j: (j, 0)),
        ],
        out_specs=pl.BlockSpec((tm, d), lambda i, j: (i, 0)),
        out_shape=jax.ShapeDtypeStruct((n, d), F32),
        scratch_shapes=[pltpu.VMEM((tm, d), BF16), pltpu.VMEM((tm, d), F32)],
        compiler_params=_cparams(("parallel", "arbitrary")),
        name="mlp",
    )(x2d, g, w_up, w_down)


def _pad_rows_front(a, rows):
    pad = rows - a.shape[1]
    return jnp.pad(a, ((0, 0), (pad, 0), (0, 0)))


def _split3_bf16(v):
    hi = v.astype(BF16)
    mid = (v - hi.astype(F32)).astype(BF16)
    lo = (v - hi.astype(F32) - mid.astype(F32)).astype(BF16)
    return hi, mid, lo


def _layer_weights(l, norm1_g, w_in, qn_g, kn_g, sb_bias, w_conv_b, w_conv_c, a_log, dt_bias, onorm_g,
                   w_pa, w_pb, w_pc, w_o, norm2_g, w_up, w_down):
    d = w_in.shape[1]
    main_w = 3 * SB_WIDTH + 3 * SC_WIDTH + 4 * GDN_WIDTH
    small_w = 2 * GDN_HEADS
    wl = w_in[l]
    w_main = jnp.concatenate([wl[:, :3 * SB_WIDTH], wl[:, main_w + small_w:], wl[:, 3 * SB_WIDTH:main_w]],
                             axis=1).astype(BF16)
    w_small = jnp.pad(wl[:, main_w:main_w + small_w], ((0, 0), (0, LANES - small_w))).astype(BF16)
    head_of = jnp.arange(SB_WIDTH) // SB_HEAD_DIM
    pmat = (head_of[:, None] == head_of[None, :]).astype(F32) / SB_HEAD_DIM
    bias2 = sb_bias[l].astype(F32) * LOG2E
    nega = -jnp.exp(a_log[l].astype(F32))
    zeros4 = jnp.zeros((GDN_HEADS,), F32)
    pad_lanes = lambda v: jnp.pad(v, (0, LANES - v.shape[0]))[None, :]
    return dict(
        g1=norm1_g[l][None, :], w_main=w_main, w_small=w_small,
        qg=jnp.tile(qn_g[l], SB_HEADS)[None, :], kg=jnp.tile(kn_g[l], SB_HEADS)[None, :],
        pmat=pmat.astype(BF16), bias2=bias2,
        wcb=jnp.pad(w_conv_b[l], ((0, SUBLANES - SC_TAPS), (0, 0))),
        wcc=jnp.pad(w_conv_c[l], ((0, SUBLANES - GDN_TAPS), (0, 0))),
        nega_row=pad_lanes(jnp.concatenate([zeros4, nega])),
        dtb_row=pad_lanes(jnp.concatenate([zeros4, dt_bias[l].astype(F32)])),
        og_row=onorm_g[l][None, :],
        wpa=w_pa[l].astype(BF16), wpb=w_pb[l].astype(BF16), wpc=w_pc[l].astype(BF16),
        wo=w_o[l].astype(BF16), g2=norm2_g[l][None, :],
        w_up=w_up[l].astype(BF16), w_down=w_down[l].astype(BF16),
    )


def _prompt_attention(q, k, v, bias2):
    nb, t, _ = q.shape
    npair = SB_HEADS // 2
    hi, mid, lo = _split3_bf16(bias2)
    parts = jnp.stack([hi, mid, lo], axis=-1).astype(F32).reshape(npair, 2, 3)
    parts = jnp.pad(parts, ((0, 0), (0, 0), (0, SB_HEAD_DIM - 3)))
    zeros = jnp.zeros_like(parts[:, 0])
    bias_rows = jnp.stack([jnp.concatenate([zeros, parts[:, 0]], axis=-1),
                           jnp.concatenate([parts[:, 1], zeros], axis=-1)], axis=1)
    k_t = k.astype(BF16).reshape(nb, t, npair, 2, SB_HEAD_DIM).transpose(0, 2, 3, 4, 1)
    ones_rows = jnp.concatenate([jnp.ones((nb, npair, 3, t), BF16),
                                 jnp.zeros((nb, npair, SB_HEAD_DIM - 3, t), BF16)], axis=2)
    k_aug = jnp.stack([jnp.concatenate([k_t[:, :, 0], ones_rows], axis=2),
                       jnp.concatenate([ones_rows, k_t[:, :, 1]], axis=2)], axis=2)
    idx = jnp.arange(SB_TK)
    umat = (idx[:, None] >= idx[None, :]).astype(BF16)
    o = _sb_prompt(q, k_aug.reshape(nb * npair, 2, LANES, t), v.astype(BF16), bias_rows, umat)
    return o.reshape(nb * t, SB_WIDTH)


def _decode_attention(q, k, v, bias2, page_ids, cache_k2, cache_v2):
    nb, tq, _ = q.shape
    nrow = SB_HEADS * tq
    q_rep = jnp.tile(q.astype(F32), (1, SB_HEADS, 1)).reshape(nb, SB_HEADS, tq, SB_WIDTH)
    head_cols = (jnp.arange(SB_WIDTH)[None, :] // SB_HEAD_DIM) == jnp.arange(SB_HEADS)[:, None]
    q_rows = jnp.where(head_cols[None, :, None, :], q_rep, 0.0).reshape(nb, nrow, SB_WIDTH)
    bias_col = jnp.repeat(bias2, tq)[:, None]
    idx = jnp.arange(2 * PAGE_SIZE)
    umat = (idx[:, None] >= idx[None, :]).astype(BF16)
    as_page = lambda a: jnp.pad(a.transpose(0, 2, 1), ((0, 0), (0, 0), (0, PAGE_SIZE - tq)))
    o = _sb_decode(page_ids, q_rows, as_page(k), as_page(v), bias_col, umat, cache_k2, cache_v2)
    o = o.reshape(nb, SB_HEADS, tq, SB_HEAD_DIM).transpose(0, 2, 1, 3)
    return o.reshape(nb * tq, SB_WIDTH)


def _trunk_layer(x3, w, attend, buf_b, buf_c, s0, chunk, n_valid):
    nb, t, d = x3.shape
    n = nb * t
    tm = min(ROW_TILE, n)
    q, k, v, rest, small = _proj(x3.reshape(n, d), w["g1"], w["w_main"], w["w_small"],
                                 w["qg"], w["kg"], w["pmat"], min(PROJ_ROW_TILE, n))
    out_a = attend(q.reshape(nb, t, SB_WIDTH), k.reshape(nb, t, SB_WIDTH), v.reshape(nb, t, SB_WIDTH))
    rest3 = rest.reshape(nb, t, rest.shape[1])
    out_b, tail_b = _convb(rest3, _pad_rows_front(buf_b, SUBLANES), w["wcb"], min(512, t))
    tg = -(-t // GDN_CHUNK) * GDN_CHUNK
    pad_t = lambda a: jnp.pad(a, ((0, 0), (0, tg - t), (0, 0)))
    out_c, s_new = _gdn(pad_t(rest3), pad_t(small.reshape(nb, t, LANES)), _pad_rows_front(buf_c, SUBLANES), s0,
                        w["wcc"], w["nega_row"], w["dtb_row"], w["og_row"], min(chunk, tg), n_valid)
    out_c = out_c[:, :t]
    x1 = _merge(x3.reshape(n, d), out_a, out_b.reshape(n, SC_WIDTH), out_c.reshape(n, GDN_WIDTH), rest,
                w["wpa"], w["wpb"], w["wpc"], w["wo"], tm)
    x2 = _mlp(x1, w["g2"], w["w_up"], w["w_down"], tm, 1024)
    nv = n_valid - (t - SUBLANES)
    new_b = tail_b[:, nv - (SC_TAPS - 1):nv]
    new_c = rest3[:, n_valid - (GDN_TAPS - 1):n_valid, REST_QKV:REST_QKV + 3 * GDN_WIDTH]
    return x2.reshape(nb, t, d), k, v, new_b, new_c, s_new


def kernel(x_prompt, x_sample, cache_k, cache_v, page_table, state_conv_b, state_conv_c, state_gdn,
           norm1_g, w_in, qn_g, kn_g, sb_bias, w_conv_b, w_conv_c, a_log, dt_bias, onorm_g,
           w_pa, w_pb, w_pc, w_o, norm2_g, w_up, w_down):
    depth = w_in.shape[0]
    nbp, tp, d = x_prompt.shape
    nbs, ts, _ = x_sample.shape
    n_phys = cache_k.shape[1]
    assert ts <= DEC_TQ and ts >= GDN_TAPS - 1 and tp % SB_TQ == 0
    assert page_table.shape[1] % DEC_PAGES_PER_STEP == 0 and cache_k.shape[2] == PAGE_SIZE

    xp = x_prompt
    xs = jnp.pad(x_sample, ((0, 0), (0, DEC_TQ - ts), (0, 0)))
    cache_k2 = cache_k.transpose(0, 1, 3, 4, 2).reshape(depth * n_phys, SB_WIDTH, PAGE_SIZE)
    cache_v2 = cache_v.transpose(0, 1, 3, 4, 2).reshape(depth * n_phys, SB_WIDTH, PAGE_SIZE)

    outs_p, outs_s = [], []
    for l in range(depth):
        w = _layer_weights(l, norm1_g, w_in, qn_g, kn_g, sb_bias, w_conv_b, w_conv_c, a_log, dt_bias,
                           onorm_g, w_pa, w_pb, w_pc, w_o, norm2_g, w_up, w_down)
        zb = jnp.zeros((nbp, SC_TAPS - 1, SC_WIDTH), F32)
        zc = jnp.zeros((nbp, GDN_TAPS - 1, 3 * GDN_WIDTH), F32)
        zs = jnp.zeros((nbp, GDN_HEADS, GDN_HEAD_DIM, GDN_HEAD_DIM), F32)
        attend_p = functools.partial(_prompt_attention, bias2=w["bias2"])
        xp, k_p, v_p, cb_p, cc_p, s_p = _trunk_layer(xp, w, attend_p, zb, zc, zs, GDN_ROWS_PER_STEP, tp)
        page_ids = page_table.astype(jnp.int32) + l * n_phys
        attend_s = functools.partial(_decode_attention, bias2=w["bias2"], page_ids=page_ids,
                                     cache_k2=cache_k2, cache_v2=cache_v2)
        xs, k_s, v_s, cb_s, cc_s, s_s = _trunk_layer(xs, w, attend_s, state_conv_b[l], state_conv_c[l],
                                                     state_gdn[l], GDN_ROWS_PER_STEP, ts)
        outs_p.append((k_p.reshape(nbp, tp, SB_HEADS, SB_HEAD_DIM), v_p.reshape(nbp, tp, SB_HEADS, SB_HEAD_DIM),
                       cb_p, cc_p, s_p))
        outs_s.append((k_s.reshape(nbs, DEC_TQ, SB_HEADS, SB_HEAD_DIM)[:, :ts],
                       v_s.reshape(nbs, DEC_TQ, SB_HEADS, SB_HEAD_DIM)[:, :ts], cb_s, cc_s, s_s))

    stack = lambda outs, i: jnp.stack([o[i] for o in outs])
    return (xp, xs[:, :ts],
            stack(outs_p, 0), stack(outs_p, 1), stack(outs_p, 2), stack(outs_p, 3), stack(outs_p, 4),
            stack(outs_s, 0), stack(outs_s, 1), stack(outs_s, 2), stack(outs_s, 3), stack(outs_s, 4))
```

```python
import functools
import math

import jax
import jax.numpy as jnp
from jax import lax
from jax.experimental import pallas as pl
from jax.experimental.pallas import tpu as pltpu

F32 = jnp.float32
BF16 = jnp.bfloat16
HIGHEST = lax.Precision.HIGHEST

EPS = 1e-6
LOG2E = 1.4426950408889634

SB_HEADS = 8
SB_HEAD_DIM = 64
SB_WIDTH = SB_HEADS * SB_HEAD_DIM
SC_WIDTH = 512
SC_TAPS = 3
GDN_HEADS = 4
GDN_HEAD_DIM = 128
GDN_WIDTH = GDN_HEADS * GDN_HEAD_DIM
GDN_TAPS = 4
GDN_CHUNK = 64
GDN_ROWS_PER_STEP = 4 * GDN_CHUNK
PAGE_SIZE = 128
N_BRANCH = 3
D_MODEL = 1024

REST_GATES = 0
REST_BCX = REST_GATES + N_BRANCH * D_MODEL
REST_QKV = REST_BCX + 3 * SC_WIDTH
REST_Z = REST_QKV + 3 * GDN_WIDTH

LANES = 128
SUBLANES = 8
COL_TILE = 512
ROW_TILE = 512
MLP_ROW_TILE = 512
MLP_FF_TILE = 1024
PROJ_ROW_TILE = 1024
VMEM_LIMIT = 48 * 1024 * 1024

SB_TQ = 512
SB_TK = 256
MASKED_LOGIT = -1e30
DEC_TQ = 8
DEC_PAGES_PER_STEP = 16


def _cparams(sem):
    return pltpu.CompilerParams(dimension_semantics=sem, vmem_limit_bytes=VMEM_LIMIT)


def _dot(a, b, precision=None):
    return jnp.dot(a, b, preferred_element_type=F32, precision=precision)


def _dot_nt(a, b, precision=None):
    return lax.dot_general(a, b, (((1,), (1,)), ((), ())), preferred_element_type=F32, precision=precision)


def _dot_tn(a, b, precision=None):
    return lax.dot_general(a, b, (((0,), (0,)), ((), ())), preferred_element_type=F32, precision=precision)


def _softplus2(z):
    return jnp.maximum(z, 0.0) + jnp.log(1.0 + jnp.exp2(-jnp.abs(z))) * LOG2E


def _proj_kernel(x_ref, g_ref, w_ref, ws_ref, qg_ref, kg_ref, p_ref,
                 q_ref, k_ref, v_ref, rest_ref, small_ref, h_ref, *, q_scale):
    j = pl.program_id(1)

    @pl.when(j == 0)
    def _():
        x = x_ref[...]
        ms = jnp.mean(x * x, axis=-1, keepdims=True)
        h = (x * lax.rsqrt(ms + EPS) * g_ref[...]).astype(BF16)
        h_ref[...] = h
        small_ref[...] = _dot(h, ws_ref[...])

    y = _dot(h_ref[...], w_ref[...])

    def head_norm(y, gain):
        sq = y * y
        hi = sq.astype(BF16)
        lo = (sq - hi.astype(F32)).astype(BF16)
        ms = _dot(hi, p_ref[...]) + _dot(lo, p_ref[...])
        return y * lax.rsqrt(ms + EPS) * gain

    @pl.when(j == 0)
    def _():
        q_ref[...] = (head_norm(y, qg_ref[...]) * q_scale).astype(BF16)

    @pl.when(j == 1)
    def _():
        k_ref[...] = head_norm(y, kg_ref[...])

    @pl.when(j == 2)
    def _():
        v_ref[...] = y

    @pl.when(j >= 3)
    def _():
        rest_ref[...] = y


def _proj(x2d, g, w_main, w_small, qg, kg, pmat, tm):
    n, d = x2d.shape
    ncol = w_main.shape[1] // COL_TILE
    nrest = w_main.shape[1] - 3 * COL_TILE
    kern = functools.partial(_proj_kernel, q_scale=SB_HEAD_DIM ** -0.5 * LOG2E)
    row = lambda i, j: (i, 0)
    const = lambda i, j: (0, 0)
    return pl.pallas_call(
        kern,
        grid=(n // tm, ncol),
        in_specs=[
            pl.BlockSpec((tm, d), row),
            pl.BlockSpec((1, d), const),
            pl.BlockSpec((d, COL_TILE), lambda i, j: (0, j)),
            pl.BlockSpec((d, LANES), const),
            pl.BlockSpec((1, COL_TILE), const),
            pl.BlockSpec((1, COL_TILE), const),
            pl.BlockSpec((COL_TILE, COL_TILE), const),
        ],
        out_specs=[
            pl.BlockSpec((tm, COL_TILE), row),
            pl.BlockSpec((tm, COL_TILE), row),
            pl.BlockSpec((tm, COL_TILE), row),
            pl.BlockSpec((tm, COL_TILE), lambda i, j: (i, jnp.maximum(j - 3, 0))),
            pl.BlockSpec((tm, LANES), row),
        ],
        out_shape=[
            jax.ShapeDtypeStruct((n, COL_TILE), BF16),
            jax.ShapeDtypeStruct((n, COL_TILE), F32),
            jax.ShapeDtypeStruct((n, COL_TILE), F32),
            jax.ShapeDtypeStruct((n, nrest), F32),
            jax.ShapeDtypeStruct((n, LANES), F32),
        ],
        scratch_shapes=[pltpu.VMEM((tm, d), BF16)],
        compiler_params=_cparams(("parallel", "arbitrary")),
        name="proj",
    )(x2d, g, w_main, w_small, qg, kg, pmat)


def _sb_prompt_kernel(q_ref, k_ref, v_ref, bias_ref, u_ref, o_ref,
                      qa_ref, z_ref, sp_ref, a_ref, m0_ref, acc_ref, r_ref):
    qi = pl.program_id(2)
    tq = q_ref.shape[1]
    tk = u_ref.shape[0]
    ratio = tq // tk
    n_tiles = ratio * (qi + 1)
    pair = (0, 1)

    lane = lax.broadcasted_iota(jnp.int32, (tq, LANES), 1)
    for e in pair:
        own = (lane < SB_HEAD_DIM) if e == 0 else (lane >= SB_HEAD_DIM)
        qa_ref[e] = jnp.where(own, q_ref[0], bias_ref[0, e:e + 1, :].astype(BF16))

    def key_start(n):
        return pl.multiple_of((n_tiles - 1 - n) * tk, tk)

    def scores(n, e, slot, diag):
        kt = k_ref[0, e, :, pl.ds(key_start(n), tk)]
        z = _dot(qa_ref[e], kt)
        sp = _softplus2(z)
        if diag:
            col = lax.broadcasted_iota(jnp.int32, (tq, tk), 1) + (tq - (n + 1) * tk)
            valid = col < lax.broadcasted_iota(jnp.int32, (tq, tk), 0)
            sp = jnp.where(valid, sp, 0.0)
            z = jnp.where(valid, z, MASKED_LOGIT)
        z_ref[e, slot] = z
        sp_ref[e, slot] = sp.astype(BF16)

    def weights(e, slot):
        m = _dot(sp_ref[e, slot], u_ref[...])
        a_ref[e, slot] = jnp.exp2(z_ref[e, slot] - m).astype(BF16)
        m0_ref[e, slot] = m[:, 0:1]

    def accumulate(n, e, slot):
        vt = v_ref[0, pl.ds(key_start(n), tk), :]
        pv = _dot(a_ref[e, slot], vt)
        r = r_ref[e]
        acc_ref[e] += jnp.exp2(r) * pv
        r_ref[e] = r - m0_ref[e, slot]

    acc_ref[...] = jnp.zeros_like(acc_ref)
    r_ref[...] = jnp.zeros_like(r_ref)
    for e in pair:
        scores(0, e, 0, True)
    for e in pair:
        scores(1, e, 1, True)
    for e in pair:
        weights(e, 0)

    def body(p, carry):
        s = 2 * p
        for e in pair:
            accumulate(s, e, 0)
            weights(e, 1)
            scores(s + 2, e, 0, False)
        for e in pair:
            accumulate(s + 1, e, 1)
            weights(e, 0)
            scores(s + 3, e, 1, False)
        return carry

    lax.fori_loop(0, qi, body, 0)
    for e in pair:
        accumulate(n_tiles - 2, e, 0)
    for e in pair:
        weights(e, 1)
    for e in pair:
        accumulate(n_tiles - 1, e, 1)
    o_ref[0] = jnp.where(lane < SB_HEAD_DIM, acc_ref[0], acc_ref[1]).astype(o_ref.dtype)


def _sb_prompt(q, k_aug, v, bias_rows, umat):
    nb, t, _ = q.shape
    npair = SB_HEADS // 2
    assert SB_TQ == 2 * SB_TK and 2 * SB_HEAD_DIM == LANES
    return pl.pallas_call(
        _sb_prompt_kernel,
        grid=(nb, npair, t // SB_TQ),
        in_specs=[
            pl.BlockSpec((1, SB_TQ, LANES), lambda b, p, i: (b, i, p)),
            pl.BlockSpec((1, 2, LANES, t), lambda b, p, i: (b * npair + p, 0, 0, 0)),
            pl.BlockSpec((1, t, LANES), lambda b, p, i: (b, 0, p)),
            pl.BlockSpec((1, 2, LANES), lambda b, p, i: (p, 0, 0)),
            pl.BlockSpec((SB_TK, SB_TK), lambda b, p, i: (0, 0)),
        ],
        out_specs=pl.BlockSpec((1, SB_TQ, LANES), lambda b, p, i: (b, i, p)),
        out_shape=jax.ShapeDtypeStruct((nb, t, SB_WIDTH), BF16),
        scratch_shapes=[
            pltpu.VMEM((2, SB_TQ, LANES), BF16),
            pltpu.VMEM((2, 2, SB_TQ, SB_TK), F32),
            pltpu.VMEM((2, 2, SB_TQ, SB_TK), BF16),
            pltpu.VMEM((2, 2, SB_TQ, SB_TK), BF16),
            pltpu.VMEM((2, 2, SB_TQ, 1), F32),
            pltpu.VMEM((2, SB_TQ, LANES), F32),
            pltpu.VMEM((2, SB_TQ, 1), F32),
        ],
        compiler_params=_cparams(("parallel", "parallel", "arbitrary")),
        name="sb_prompt",
    )(q, k_aug, v, bias_rows, umat)


def _sb_decode_kernel(pt_ref, q_ref, kn_ref, vn_ref, bias_ref, u_ref, *rest):
    del pt_ref
    npg = DEC_PAGES_PER_STEP
    kp_refs = rest[:npg]
    vp_refs = rest[npg:2 * npg]
    o_ref, acc_ref, r_ref = rest[2 * npg:]
    g = pl.program_id(1)
    nh, tq = SB_HEADS, DEC_TQ

    def side_by_side(refs):
        return refs[0][0] if len(refs) == 1 else jnp.concatenate([ref[0] for ref in refs], axis=1)

    def scores(k_refs, valid):
        z = _dot(q_ref[0], side_by_side(k_refs)) + bias_ref[...]
        sp = _softplus2(z)
        if valid is not None:
            sp = jnp.where(valid, sp, 0.0)
            z = jnp.where(valid, z, MASKED_LOGIT)
        nk = z.shape[1]
        m = _dot(sp.astype(BF16), u_ref[0:nk, 0:nk])
        return z - m, m[:, 0:1]

    def attend(blocks):
        scored = [scores(k_refs, valid) for k_refs, _, valid in blocks]
        r = r_ref[...]
        out = None
        for (e, m0), (_, v_refs, _) in zip(scored, blocks):
            a = jnp.exp2(e + r)
            pv = _dot_nt(a, side_by_side(v_refs))
            out = pv if out is None else out + pv
            r = r - m0
        r_ref[...] = r
        acc_ref[...] += out

    pages = [([kp_refs[i], kp_refs[i + 1]], [vp_refs[i], vp_refs[i + 1]], None)
             for i in reversed(range(0, npg, 2))]

    @pl.when(g == 0)
    def _():
        acc_ref[...] = jnp.zeros_like(acc_ref)
        r_ref[...] = jnp.zeros_like(r_ref)
        key = lax.broadcasted_iota(jnp.int32, (nh * tq, PAGE_SIZE), 1)
        qry = lax.broadcasted_iota(jnp.int32, (nh * tq, PAGE_SIZE), 0) % tq
        attend([([kn_ref], [vn_ref], key < qry)])

    attend(pages)

    @pl.when(g == pl.num_programs(1) - 1)
    def _():
        acc = acc_ref[...]
        rhead = lax.broadcasted_iota(jnp.int32, acc.shape, 0) // tq
        chead = lax.broadcasted_iota(jnp.int32, acc.shape, 1) // SB_HEAD_DIM
        picked = jnp.where(rhead == chead, acc, 0.0)
        folded = picked[:, 0:SB_HEAD_DIM]
        for h in range(1, nh):
            folded = folded + picked[:, h * SB_HEAD_DIM:(h + 1) * SB_HEAD_DIM]
        o_ref[0] = folded.astype(o_ref.dtype)


def _sb_decode(page_ids, q_rows, k_new, v_new, bias_col, umat, cache_k3, cache_v3):
    nb, n_pages = page_ids.shape
    npg = DEC_PAGES_PER_STEP
    ngroups = n_pages // npg
    nrow = SB_HEADS * DEC_TQ

    def page_spec(i):
        return pl.BlockSpec((1, SB_WIDTH, PAGE_SIZE),
                            lambda b, g, pt: (pt[b, (ngroups - 1 - g) * npg + i], 0, 0))

    per_b = lambda b, g, pt: (b, 0, 0)
    const = lambda b, g, pt: (0, 0)
    grid_spec = pltpu.PrefetchScalarGridSpec(
        num_scalar_prefetch=1,
        grid=(nb, ngroups),
        in_specs=[
            pl.BlockSpec((1, nrow, SB_WIDTH), per_b),
            pl.BlockSpec((1, SB_WIDTH, PAGE_SIZE), per_b),
            pl.BlockSpec((1, SB_WIDTH, PAGE_SIZE), per_b),
            pl.BlockSpec((nrow, 1), const),
            pl.BlockSpec((2 * PAGE_SIZE, 2 * PAGE_SIZE), const),
        ] + [page_spec(i) for i in range(npg)] * 2,
        out_specs=pl.BlockSpec((1, nrow, SB_HEAD_DIM), per_b),
        scratch_shapes=[
            pltpu.VMEM((nrow, SB_WIDTH), F32),
            pltpu.VMEM((nrow, 1), F32),
        ],
    )
    return pl.pallas_call(
        _sb_decode_kernel,
        grid_spec=grid_spec,
        out_shape=jax.ShapeDtypeStruct((nb, nrow, SB_HEAD_DIM), BF16),
        compiler_params=_cparams(("parallel", "arbitrary")),
        name="sb_decode",
    )(page_ids, q_rows, k_new, v_new, bias_col, umat, *([cache_k3] * npg), *([cache_v3] * npg))


def _convb_kernel(b_ref, c_ref, x_ref, cp_ref, xp_ref, buf_ref, w_ref, o_ref, tail_ref, u_ref):
    i = pl.program_id(1)
    tt = b_ref.shape[1]
    u = c_ref[0] * x_ref[0]
    prev = jnp.where(i == 0, buf_ref[0], cp_ref[0] * xp_ref[0])
    u_ref[0:SUBLANES, :] = prev
    u_ref[SUBLANES:, :] = u
    base = SUBLANES - (SC_TAPS - 1)
    y = w_ref[0:1, :] * u_ref[base:base + tt, :]
    for j in range(1, SC_TAPS):
        y = y + w_ref[j:j + 1, :] * u_ref[base + j:base + j + tt, :]
    o_ref[0] = (b_ref[0] * y).astype(o_ref.dtype)
    tail_ref[0] = u[tt - SUBLANES:, :]


def _convb(rest3, buf8, w8, tt):
    nb, t, _ = rest3.shape
    r8 = tt // SUBLANES
    c0 = REST_BCX // COL_TILE
    cur = lambda c: pl.BlockSpec((1, tt, COL_TILE), lambda b, i: (b, i, c))
    prev = lambda c: pl.BlockSpec((1, SUBLANES, COL_TILE), lambda b, i: (b, jnp.maximum(i * r8 - 1, 0), c))
    return pl.pallas_call(
        _convb_kernel,
        grid=(nb, t // tt),
        in_specs=[cur(c0), cur(c0 + 1), cur(c0 + 2), prev(c0 + 1), prev(c0 + 2),
                  pl.BlockSpec((1, SUBLANES, COL_TILE), lambda b, i: (b, 0, 0)),
                  pl.BlockSpec((SUBLANES, COL_TILE), lambda b, i: (0, 0))],
        out_specs=[pl.BlockSpec((1, tt, COL_TILE), lambda b, i: (b, i, 0)),
                   pl.BlockSpec((1, SUBLANES, COL_TILE), lambda b, i: (b, 0, 0))],
        out_shape=[jax.ShapeDtypeStruct((nb, t, COL_TILE), BF16),
                   jax.ShapeDtypeStruct((nb, SUBLANES, COL_TILE), F32)],
        scratch_shapes=[pltpu.VMEM((tt + SUBLANES, COL_TILE), F32)],
        compiler_params=_cparams(("parallel", "arbitrary")),
        name="convb",
    )(rest3, rest3, rest3, rest3, rest3, buf8, w8)


def _split_bf16(x):
    hi = x.astype(BF16)
    return hi, (x - hi.astype(F32)).astype(BF16)


def _dot3(a, b, nt=False):
    ah, al = _split_bf16(a)
    bh, bl = _split_bf16(b)
    d = _dot_nt if nt else _dot
    return d(ah, bh) + (d(al, bh) + d(ah, bl))


def _dotb(a, b):
    return _dot(a.astype(BF16), b.astype(BF16))


def _gdn_kernel(x_ref, z_ref, sm_ref, xp_ref, buf_ref, s0_ref, w_ref, nega_ref, dtb_ref, og_ref,
                o_ref, s_out_ref, s_ref, xb_ref, *, n_valid):
    i = pl.program_id(1)
    rows = x_ref.shape[1]
    c = GDN_CHUNK
    nh = GDN_HEADS
    hd = GDN_HEAD_DIM

    @pl.when(i == 0)
    def _():
        s_ref[...] = s0_ref[0]

    xb_ref[0:SUBLANES, :] = jnp.where(i == 0, buf_ref[0], xp_ref[0])
    xb_ref[SUBLANES:, :] = x_ref[0]
    base = SUBLANES - (GDN_TAPS - 1)
    y = w_ref[0:1, :] * xb_ref[base:base + rows, :]
    for j in range(1, GDN_TAPS):
        y = y + w_ref[j:j + 1, :] * xb_ref[base + j:base + j + rows, :]
    act = y * jax.nn.sigmoid(y)

    sm = sm_ref[0]
    beta_all = jax.nn.sigmoid(sm)
    pre = sm + dtb_ref[...]
    g_all = nega_ref[...] * (jnp.maximum(pre, 0.0) + jnp.log1p(jnp.exp(-jnp.abs(pre))))
    if n_valid < rows:
        live = lax.broadcasted_iota(jnp.int32, (rows, LANES), 0) < n_valid
        beta_all = jnp.where(live, beta_all, 0.0)
        g_all = jnp.where(live, g_all, 0.0)

    rr = lax.broadcasted_iota(jnp.int32, (rows, rows), 0)
    cc = lax.broadcasted_iota(jnp.int32, (rows, rows), 1)
    gcum = _dot3(((rr // c == cc // c) & (cc <= rr)).astype(F32), g_all)

    row = lax.broadcasted_iota(jnp.int32, (c, c), 0)
    col = lax.broadcasted_iota(jnp.int32, (c, c), 1)
    incl = col <= row
    strict = col < row
    wide = nh * c
    eye_w = (lax.broadcasted_iota(jnp.int32, (c, wide), 0)
             == lax.broadcasted_iota(jnp.int32, (c, wide), 1) % c).astype(F32)
    same_block = (lax.broadcasted_iota(jnp.int32, (wide, wide), 0) // c
                  == lax.broadcasted_iota(jnp.int32, (wide, wide), 1) // c)

    def times_blockdiag(x, p):
        xh, xl = _split_bf16(x)
        ph, pl_ = _split_bf16(p)
        zero = jnp.zeros((), BF16)
        bdh = jnp.where(same_block, jnp.tile(ph, (nh, 1)), zero)
        bdl = jnp.where(same_block, jnp.tile(pl_, (nh, 1)), zero)
        return _dot(xh, bdh) + (_dot(xl, bdh) + _dot(xh, bdl))

    nchunks = rows // c
    chunks = []
    for ci in range(nchunks):
        r0 = ci * c
        heads = []
        gcum_t = gcum[r0:r0 + c, :].T
        for h in range(nh):
            qh = act[r0:r0 + c, h * hd:(h + 1) * hd]
            kh = act[r0:r0 + c, GDN_WIDTH + h * hd:GDN_WIDTH + (h + 1) * hd]
            vh = act[r0:r0 + c, 2 * GDN_WIDTH + h * hd:2 * GDN_WIDTH + (h + 1) * hd]
            qh = qh * lax.rsqrt(jnp.sum(qh * qh, axis=-1, keepdims=True) + EPS) * (hd ** -0.5)
            kh = kh * lax.rsqrt(jnp.sum(kh * kh, axis=-1, keepdims=True) + EPS)
            b = beta_all[r0:r0 + c, h:h + 1]
            gc = gcum[r0:r0 + c, nh + h:nh + h + 1]
            d = gc - gcum_t[nh + h:nh + h + 1, :]
            decay = jnp.where(incl, jnp.exp(jnp.where(incl, d, 0.0)), 0.0)
            a = jnp.where(strict, b * decay * _dot3(kh, kh, nt=True), 0.0)
            heads.append((qh, kh, vh, b, gc, decay, a))
        chunks.append(heads)

    ps = [jnp.concatenate([hh[6] for hh in heads], axis=1) for heads in chunks]
    tinvs = [eye_w - p for p in ps]
    for _ in range(int(math.log2(c)) - 1):
        ps = [times_blockdiag(p, p) for p in ps]
        tinvs = [t + times_blockdiag(t, p) for t, p in zip(tinvs, ps)]

    def prepare(ci):
        per_head = []
        for h, (qh, kh, vh, b, gc, decay, _) in enumerate(chunks[ci]):
            th = tinvs[ci][:, h * c:(h + 1) * c]
            eg = jnp.exp(gc)
            g_last = gc[c - 1:c, :]
            u = _dotb(th, b * vh)
            wk = _dotb(th, (b * eg) * kh).astype(BF16)
            qk = jnp.where(incl, _dot_nt(qh.astype(BF16), kh.astype(BF16)) * decay, 0.0).astype(BF16)
            per_head.append((u, wk, qk, (eg * qh).astype(BF16),
                             (kh * jnp.exp(g_last - gc)).astype(BF16), jnp.exp(g_last)))
        return per_head

    state = [s_ref[h] for h in range(nh)]
    per_head = prepare(0)
    for ci in range(nchunks):
        r0 = ci * c
        for h, (u, wk, qk, egq, kd, dec_last) in enumerate(per_head):
            s = state[h]
            sb = s.astype(BF16)
            w = u - _dot(wk, sb)
            wb = w.astype(BF16)
            o = _dot(egq, sb) + _dot(qk, wb)
            state[h] = dec_last * s + _dot_tn(kd, wb)
            on = o * lax.rsqrt(jnp.mean(o * o, axis=-1, keepdims=True) + EPS) * og_ref[...]
            zh = z_ref[0, r0:r0 + c, h * hd:(h + 1) * hd]
            o_ref[0, r0:r0 + c, h * hd:(h + 1) * hd] = (on * (zh * jax.nn.sigmoid(zh))).astype(o_ref.dtype)
        if ci + 1 < nchunks:
            per_head = prepare(ci + 1)
    for h in range(nh):
        s_ref[h] = state[h]

    @pl.when(i == pl.num_programs(1) - 1)
    def _():
        s_out_ref[0] = s_ref[...]


def _gdn(rest3, small3, buf8, s0, w8, nega_row, dtb_row, og_row, c, n_valid):
    nb, t, _ = rest3.shape
    r8 = c // SUBLANES
    qkv_w = 3 * GDN_WIDTH
    kern = functools.partial(_gdn_kernel, n_valid=n_valid)
    const2 = lambda b, i: (0, 0)
    return pl.pallas_call(
        kern,
        grid=(nb, t // c),
        in_specs=[
            pl.BlockSpec((1, c, qkv_w), lambda b, i: (b, i, REST_QKV // qkv_w)),
            pl.BlockSpec((1, c, GDN_WIDTH), lambda b, i: (b, i, REST_Z // GDN_WIDTH)),
            pl.BlockSpec((1, c, LANES), lambda b, i: (b, i, 0)),
            pl.BlockSpec((1, SUBLANES, qkv_w), lambda b, i: (b, jnp.maximum(i * r8 - 1, 0), REST_QKV // qkv_w)),
            pl.BlockSpec((1, SUBLANES, qkv_w), lambda b, i: (b, 0, 0)),
            pl.BlockSpec((1, GDN_HEADS, GDN_HEAD_DIM, GDN_HEAD_DIM), lambda b, i: (b, 0, 0, 0)),
            pl.BlockSpec((SUBLANES, qkv_w), const2),
            pl.BlockSpec((1, LANES), const2),
            pl.BlockSpec((1, LANES), const2),
            pl.BlockSpec((1, GDN_HEAD_DIM), const2),
        ],
        out_specs=[
            pl.BlockSpec((1, c, GDN_WIDTH), lambda b, i: (b, i, 0)),
            pl.BlockSpec((1, GDN_HEADS, GDN_HEAD_DIM, GDN_HEAD_DIM), lambda b, i: (b, 0, 0, 0)),
        ],
        out_shape=[
            jax.ShapeDtypeStruct((nb, t, GDN_WIDTH), BF16),
            jax.ShapeDtypeStruct((nb, GDN_HEADS, GDN_HEAD_DIM, GDN_HEAD_DIM), F32),
        ],
        scratch_shapes=[
            pltpu.VMEM((GDN_HEADS, GDN_HEAD_DIM, GDN_HEAD_DIM), F32),
            pltpu.VMEM((c + SUBLANES, qkv_w), F32),
        ],
        compiler_params=_cparams(("parallel", "arbitrary")),
        name="gdn",
    )(rest3, rest3, small3, rest3, buf8, s0, w8, nega_row, dtb_row, og_row)


def _merge_kernel(x_ref, a_ref, b_ref, c_ref, gl_ref, wpa_ref, wpb_ref, wpc_ref, wo_ref, o_ref):
    d = x_ref.shape[1]
    gl = gl_ref[...]
    mix = jax.nn.sigmoid(gl[:, 0:d]) * _dot(a_ref[...], wpa_ref[...])
    mix = mix + jax.nn.sigmoid(gl[:, d:2 * d]) * _dot(b_ref[...], wpb_ref[...])
    mix = mix + jax.nn.sigmoid(gl[:, 2 * d:3 * d]) * _dot(c_ref[...], wpc_ref[...])
    o_ref[...] = x_ref[...] + _dot(mix.astype(BF16), wo_ref[...])


def _merge(x2d, out_a, out_b, out_c, rest, wpa, wpb, wpc, wo, tm):
    n, d = x2d.shape
    row = lambda i: (i, 0)
    const = lambda i: (0, 0)
    return pl.pallas_call(
        _merge_kernel,
        grid=(n // tm,),
        in_specs=[
            pl.BlockSpec((tm, d), row),
            pl.BlockSpec((tm, COL_TILE), row),
            pl.BlockSpec((tm, COL_TILE), row),
            pl.BlockSpec((tm, COL_TILE), row),
            pl.BlockSpec((tm, N_BRANCH * d), row),
            pl.BlockSpec((COL_TILE, d), const),
            pl.BlockSpec((COL_TILE, d), const),
            pl.BlockSpec((COL_TILE, d), const),
            pl.BlockSpec((d, d), const),
        ],
        out_specs=pl.BlockSpec((tm, d), row),
        out_shape=jax.ShapeDtypeStruct((n, d), F32),
        compiler_params=_cparams(("parallel",)),
        name="merge",
    )(x2d, out_a, out_b, out_c, rest, wpa, wpb, wpc, wo)


def _mlp_kernel(x_ref, g_ref, wu_ref, wd_ref, o_ref, h_ref, acc_ref):
    j = pl.program_id(1)

    @pl.when(j == 0)
    def _():
        x = x_ref[...]
        ms = jnp.mean(x * x, axis=-1, keepdims=True)
        h_ref[...] = (x * lax.rsqrt(ms + EPS) * g_ref[...]).astype(BF16)
        acc_ref[...] = x

    u = jnp.maximum(_dot(h_ref[...], wu_ref[...]), 0.0)
    acc_ref[...] += _dot((u * u).astype(BF16), wd_ref[...])

    @pl.when(j == pl.num_programs(1) - 1)
    def _():
        o_ref[...] = acc_ref[...]


def _mlp(x2d, g, w_up, w_down, tm, tf):
    n, d = x2d.shape
    dff = w_up.shape[1]
    return pl.pallas_call(
        _mlp_kernel,
        grid=(n // tm, dff // tf),
        in_specs=[
            pl.BlockSpec((tm, d), lambda i, j: (i, 0)),
            pl.BlockSpec((1, d), lambda i, j: (0, 0)),
            pl.BlockSpec((d, tf), lambda i, j: (0, j)),
            pl.BlockSpec((tf, d), lambda i, j: (j, 0)),
        ],
        out_specs=pl.BlockSpec((tm, d), lambda i, j: (i, 0)),
        out_shape=jax.ShapeDtypeStruct((n, d), F32),
        scratch_shapes=[pltpu.VMEM((tm, d), BF16), pltpu.VMEM((tm, d), F32)],
        compiler_params=_cparams(("parallel", "arbitrary")),
        name="mlp",
    )(x2d, g, w_up, w_down)


def _pad_rows_front(a, rows):
    pad = rows - a.shape[1]
    return jnp.pad(a, ((0, 0), (pad, 0), (0, 0)))


def _split3_bf16(v):
    hi = v.astype(BF16)
    mid = (v - hi.astype(F32)).astype(BF16)
    lo = (v - hi.astype(F32) - mid.astype(F32)).astype(BF16)
    return hi, mid, lo


def _layer_weights(l, norm1_g, w_in, qn_g, kn_g, sb_bias, w_conv_b, w_conv_c, a_log, dt_bias, onorm_g,
                   w_pa, w_pb, w_pc, w_o, norm2_g, w_up, w_down):
    d = w_in.shape[1]
    main_w = 3 * SB_WIDTH + 3 * SC_WIDTH + 4 * GDN_WIDTH
    small_w = 2 * GDN_HEADS
    wl = w_in[l]
    w_main = jnp.concatenate([wl[:, :3 * SB_WIDTH], wl[:, main_w + small_w:], wl[:, 3 * SB_WIDTH:main_w]],
                             axis=1).astype(BF16)
    w_small = jnp.pad(wl[:, main_w:main_w + small_w], ((0, 0), (0, LANES - small_w))).astype(BF16)
    head_of = jnp.arange(SB_WIDTH) // SB_HEAD_DIM
    pmat = (head_of[:, None] == head_of[None, :]).astype(F32) / SB_HEAD_DIM
    bias2 = sb_bias[l].astype(F32) * LOG2E
    nega = -jnp.exp(a_log[l].astype(F32))
    zeros4 = jnp.zeros((GDN_HEADS,), F32)
    pad_lanes = lambda v: jnp.pad(v, (0, LANES - v.shape[0]))[None, :]
    return dict(
        g1=norm1_g[l][None, :], w_main=w_main, w_small=w_small,
        qg=jnp.tile(qn_g[l], SB_HEADS)[None, :], kg=jnp.tile(kn_g[l], SB_HEADS)[None, :],
        pmat=pmat.astype(BF16), bias2=bias2,
        wcb=jnp.pad(w_conv_b[l], ((0, SUBLANES - SC_TAPS), (0, 0))),
        wcc=jnp.pad(w_conv_c[l], ((0, SUBLANES - GDN_TAPS), (0, 0))),
        nega_row=pad_lanes(jnp.concatenate([zeros4, nega])),
        dtb_row=pad_lanes(jnp.concatenate([zeros4, dt_bias[l].astype(F32)])),
        og_row=onorm_g[l][None, :],
        wpa=w_pa[l].astype(BF16), wpb=w_pb[l].astype(BF16), wpc=w_pc[l].astype(BF16),
        wo=w_o[l].astype(BF16), g2=norm2_g[l][None, :],
        w_up=w_up[l].astype(BF16), w_down=w_down[l].astype(BF16),
    )


def _prompt_attention(q, k, v, bias2):
    nb, t, _ = q.shape
    npair = SB_HEADS // 2
    hi, mid, lo = _split3_bf16(bias2)
    parts = jnp.stack([hi, mid, lo], axis=-1).astype(F32).reshape(npair, 2, 3)
    parts = jnp.pad(parts, ((0, 0), (0, 0), (0, SB_HEAD_DIM - 3)))
    zeros = jnp.zeros_like(parts[:, 0])
    bias_rows = jnp.stack([jnp.concatenate([zeros, parts[:, 0]], axis=-1),
                           jnp.concatenate([parts[:, 1], zeros], axis=-1)], axis=1)
    k_t = k.astype(BF16).reshape(nb, t, npair, 2, SB_HEAD_DIM).transpose(0, 2, 3, 4, 1)
    ones_rows = jnp.concatenate([jnp.ones((nb, npair, 3, t), BF16),
                                 jnp.zeros((nb, npair, SB_HEAD_DIM - 3, t), BF16)], axis=2)
    k_aug = jnp.stack([jnp.concatenate([k_t[:, :, 0], ones_rows], axis=2),
                       jnp.concatenate([ones_rows, k_t[:, :, 1]], axis=2)], axis=2)
    idx = jnp.arange(SB_TK)
    umat = (idx[:, None] >= idx[None, :]).astype(BF16)
    o = _sb_prompt(q, k_aug.reshape(nb * npair, 2, LANES, t), v.astype(BF16), bias_rows, umat)
    return o.reshape(nb * t, SB_WIDTH)


def _decode_attention(q, k, v, bias2, page_ids, cache_k2, cache_v2):
    nb, tq, _ = q.shape
    nrow = SB_HEADS * tq
    q_rep = jnp.tile(q.astype(F32), (1, SB_HEADS, 1)).reshape(nb, SB_HEADS, tq, SB_WIDTH)
    head_cols = (jnp.arange(SB_WIDTH)[None, :] // SB_HEAD_DIM) == jnp.arange(SB_HEADS)[:, None]
    q_rows = jnp.where(head_cols[None, :, None, :], q_rep, 0.0).reshape(nb, nrow, SB_WIDTH)
    bias_col = jnp.repeat(bias2, tq)[:, None]
    idx = jnp.arange(2 * PAGE_SIZE)
    umat = (idx[:, None] >= idx[None, :]).astype(BF16)
    as_page = lambda a: jnp.pad(a.transpose(0, 2, 1), ((0, 0), (0, 0), (0, PAGE_SIZE - tq)))
    o = _sb_decode(page_ids, q_rows, as_page(k), as_page(v), bias_col, umat, cache_k2, cache_v2)
    o = o.reshape(nb, SB_HEADS, tq, SB_HEAD_DIM).transpose(0, 2, 1, 3)
    return o.reshape(nb * tq, SB_WIDTH)


def _trunk_layer(x3, w, attend, buf_b, buf_c, s0, chunk, n_valid):
    nb, t, d = x3.shape
    n = nb * t
    tm = min(ROW_TILE, n)
    q, k, v, rest, small = _proj(x3.reshape(n, d), w["g1"], w["w_main"], w["w_small"],
                                 w["qg"], w["kg"], w["pmat"], min(PROJ_ROW_TILE, n))
    out_a = attend(q.reshape(nb, t, SB_WIDTH), k.reshape(nb, t, SB_WIDTH), v.reshape(nb, t, SB_WIDTH))
    rest3 = rest.reshape(nb, t, rest.shape[1])
    out_b, tail_b = _convb(rest3, _pad_rows_front(buf_b, SUBLANES), w["wcb"], min(512, t))
    tg = -(-t // GDN_CHUNK) * GDN_CHUNK
    pad_t = lambda a: jnp.pad(a, ((0, 0), (0, tg - t), (0, 0)))
    out_c, s_new = _gdn(pad_t(rest3), pad_t(small.reshape(nb, t, LANES)), _pad_rows_front(buf_c, SUBLANES), s0,
                        w["wcc"], w["nega_row"], w["dtb_row"], w["og_row"], min(chunk, tg), n_valid)
    out_c = out_c[:, :t]
    x1 = _merge(x3.reshape(n, d), out_a, out_b.reshape(n, SC_WIDTH), out_c.reshape(n, GDN_WIDTH), rest,
                w["wpa"], w["wpb"], w["wpc"], w["wo"], tm)
    x2 = _mlp(x1, w["g2"], w["w_up"], w["w_down"], min(MLP_ROW_TILE, n), MLP_FF_TILE)
    nv = n_valid - (t - SUBLANES)
    new_b = tail_b[:, nv - (SC_TAPS - 1):nv]
    new_c = rest3[:, n_valid - (GDN_TAPS - 1):n_valid, REST_QKV:REST_QKV + 3 * GDN_WIDTH]
    return x2.reshape(nb, t, d), k, v, new_b, new_c, s_new


def kernel(x_prompt, x_sample, cache_k, cache_v, page_table, state_conv_b, state_conv_c, state_gdn,
           norm1_g, w_in, qn_g, kn_g, sb_bias, w_conv_b, w_conv_c, a_log, dt_bias, onorm_g,
           w_pa, w_pb, w_pc, w_o, norm2_g, w_up, w_down):
    depth = w_in.shape[0]
    nbp, tp, d = x_prompt.shape
    nbs, ts, _ = x_sample.shape
    n_phys = cache_k.shape[1]
    assert ts <= DEC_TQ and ts >= GDN_TAPS - 1 and tp % SB_TQ == 0
    assert page_table.shape[1] % DEC_PAGES_PER_STEP == 0 and cache_k.shape[2] == PAGE_SIZE

    xp = x_prompt
    xs = jnp.pad(x_sample, ((0, 0), (0, DEC_TQ - ts), (0, 0)))
    cache_k2 = cache_k.transpose(0, 1, 3, 4, 2).reshape(depth * n_phys, SB_WIDTH, PAGE_SIZE)
    cache_v2 = cache_v.transpose(0, 1, 3, 4, 2).reshape(depth * n_phys, SB_WIDTH, PAGE_SIZE)

    outs_p, outs_s = [], []
    for l in range(depth):
        w = _layer_weights(l, norm1_g, w_in, qn_g, kn_g, sb_bias, w_conv_b, w_conv_c, a_log, dt_bias,
                           onorm_g, w_pa, w_pb, w_pc, w_o, norm2_g, w_up, w_down)
        zb = jnp.zeros((nbp, SC_TAPS - 1, SC_WIDTH), F32)
        zc = jnp.zeros((nbp, GDN_TAPS - 1, 3 * GDN_WIDTH), F32)
        zs = jnp.zeros((nbp, GDN_HEADS, GDN_HEAD_DIM, GDN_HEAD_DIM), F32)
        attend_p = functools.partial(_prompt_attention, bias2=w["bias2"])
        xp, k_p, v_p, cb_p, cc_p, s_p = _trunk_layer(xp, w, attend_p, zb, zc, zs, GDN_ROWS_PER_STEP, tp)
        page_ids = page_table.astype(jnp.int32) + l * n_phys
        attend_s = functools.partial(_decode_attention, bias2=w["bias2"], page_ids=page_ids,
                                     cache_k2=cache_k2, cache_v2=cache_v2)
        xs, k_s, v_s, cb_s, cc_s, s_s = _trunk_layer(xs, w, attend_s, state_conv_b[l], state_conv_c[l],
                                                     state_gdn[l], GDN_ROWS_PER_STEP, ts)
        outs_p.append((k_p.reshape(nbp, tp, SB_HEADS, SB_HEAD_DIM), v_p.reshape(nbp, tp, SB_HEADS, SB_HEAD_DIM),
                       cb_p, cc_p, s_p))
        outs_s.append((k_s.reshape(nbs, DEC_TQ, SB_HEADS, SB_HEAD_DIM)[:, :ts],
                       v_s.reshape(nbs, DEC_TQ, SB_HEADS, SB_HEAD_DIM)[:, :ts], cb_s, cc_s, s_s))

    stack = lambda outs, i: jnp.stack([o[i] for o in outs])
    return (xp, xs[:, :ts],
            stack(outs_p, 0), stack(outs_p, 1), stack(outs_p, 2), stack(outs_p, 3), stack(outs_p, 4),
            stack(outs_s, 0), stack(outs_s, 1), stack(outs_s, 2), stack(outs_s, 3), stack(outs_s, 4))
```

```python
import functools
import math

import jax
import jax.numpy as jnp
from jax import lax
from jax.experimental import pallas as pl
from jax.experimental.pallas import tpu as pltpu

F32 = jnp.float32
BF16 = jnp.bfloat16
HIGHEST = lax.Precision.HIGHEST

EPS = 1e-6
LOG2E = 1.4426950408889634

SB_HEADS = 8
SB_HEAD_DIM = 64
SB_WIDTH = SB_HEADS * SB_HEAD_DIM
SC_WIDTH = 512
SC_TAPS = 3
GDN_HEADS = 4
GDN_HEAD_DIM = 128
GDN_WIDTH = GDN_HEADS * GDN_HEAD_DIM
GDN_TAPS = 4
GDN_CHUNK = 64
GDN_MIN_CHUNK = 16
GDN_ROWS_PER_STEP = 4 * GDN_CHUNK
PAGE_SIZE = 128
N_BRANCH = 3
D_MODEL = 1024

REST_GATES = 0
REST_BCX = REST_GATES + N_BRANCH * D_MODEL
REST_QKV = REST_BCX + 3 * SC_WIDTH
REST_Z = REST_QKV + 3 * GDN_WIDTH

LANES = 128
SUBLANES = 8
COL_TILE = 512
ROW_TILE = 512
MLP_ROW_TILE = 512
MLP_FF_TILE = 1024
PROJ_ROW_TILE = 1024
VMEM_LIMIT = 48 * 1024 * 1024

SB_TQ = 512
SB_TK = 256
MASKED_LOGIT = -1e30
DEC_TQ = 8
DEC_PAGES_PER_STEP = 16


def _cparams(sem):
    return pltpu.CompilerParams(dimension_semantics=sem, vmem_limit_bytes=VMEM_LIMIT)


def _dot(a, b, precision=None):
    return jnp.dot(a, b, preferred_element_type=F32, precision=precision)


def _dot_nt(a, b, precision=None):
    return lax.dot_general(a, b, (((1,), (1,)), ((), ())), preferred_element_type=F32, precision=precision)


def _dot_tn(a, b, precision=None):
    return lax.dot_general(a, b, (((0,), (0,)), ((), ())), preferred_element_type=F32, precision=precision)


def _softplus2(z):
    return jnp.maximum(z, 0.0) + jnp.log(1.0 + jnp.exp2(-jnp.abs(z))) * LOG2E


def _proj_kernel(x_ref, g_ref, w_ref, ws_ref, qg_ref, kg_ref, p_ref,
                 q_ref, k_ref, v_ref, rest_ref, small_ref, h_ref, *, q_scale):
    j = pl.program_id(1)

    @pl.when(j == 0)
    def _():
        x = x_ref[...]
        ms = jnp.mean(x * x, axis=-1, keepdims=True)
        h = (x * lax.rsqrt(ms + EPS) * g_ref[...]).astype(BF16)
        h_ref[...] = h
        small_ref[...] = _dot(h, ws_ref[...])

    y = _dot(h_ref[...], w_ref[...])

    def head_norm(y, gain):
        sq = y * y
        hi = sq.astype(BF16)
        lo = (sq - hi.astype(F32)).astype(BF16)
        ms = _dot(hi, p_ref[...]) + _dot(lo, p_ref[...])
        return y * lax.rsqrt(ms + EPS) * gain

    @pl.when(j == 0)
    def _():
        q_ref[...] = (head_norm(y, qg_ref[...]) * q_scale).astype(BF16)

    @pl.when(j == 1)
    def _():
        k_ref[...] = head_norm(y, kg_ref[...])

    @pl.when(j == 2)
    def _():
        v_ref[...] = y

    @pl.when(j >= 3)
    def _():
        rest_ref[...] = y


def _proj(x2d, g, w_main, w_small, qg, kg, pmat, tm):
    n, d = x2d.shape
    ncol = w_main.shape[1] // COL_TILE
    nrest = w_main.shape[1] - 3 * COL_TILE
    kern = functools.partial(_proj_kernel, q_scale=SB_HEAD_DIM ** -0.5 * LOG2E)
    row = lambda i, j: (i, 0)
    const = lambda i, j: (0, 0)
    return pl.pallas_call(
        kern,
        grid=(n // tm, ncol),
        in_specs=[
            pl.BlockSpec((tm, d), row),
            pl.BlockSpec((1, d), const),
            pl.BlockSpec((d, COL_TILE), lambda i, j: (0, j)),
            pl.BlockSpec((d, LANES), const),
            pl.BlockSpec((1, COL_TILE), const),
            pl.BlockSpec((1, COL_TILE), const),
            pl.BlockSpec((COL_TILE, COL_TILE), const),
        ],
        out_specs=[
            pl.BlockSpec((tm, COL_TILE), row),
            pl.BlockSpec((tm, COL_TILE), row),
            pl.BlockSpec((tm, COL_TILE), row),
            pl.BlockSpec((tm, COL_TILE), lambda i, j: (i, jnp.maximum(j - 3, 0))),
            pl.BlockSpec((tm, LANES), row),
        ],
        out_shape=[
            jax.ShapeDtypeStruct((n, COL_TILE), BF16),
            jax.ShapeDtypeStruct((n, COL_TILE), F32),
            jax.ShapeDtypeStruct((n, COL_TILE), F32),
            jax.ShapeDtypeStruct((n, nrest), F32),
            jax.ShapeDtypeStruct((n, LANES), F32),
        ],
        scratch_shapes=[pltpu.VMEM((tm, d), BF16)],
        compiler_params=_cparams(("parallel", "arbitrary")),
        name="proj",
    )(x2d, g, w_main, w_small, qg, kg, pmat)


def _sb_prompt_kernel(q_ref, k_ref, v_ref, bias_ref, u_ref, o_ref,
                      qa_ref, z_ref, sp_ref, a_ref, m0_ref, acc_ref, r_ref):
    qi = pl.program_id(2)
    tq = q_ref.shape[1]
    tk = u_ref.shape[0]
    ratio = tq // tk
    n_tiles = ratio * (qi + 1)
    pair = (0, 1)

    lane = lax.broadcasted_iota(jnp.int32, (tq, LANES), 1)
    for e in pair:
        own = (lane < SB_HEAD_DIM) if e == 0 else (lane >= SB_HEAD_DIM)
        qa_ref[e] = jnp.where(own, q_ref[0], bias_ref[0, e:e + 1, :].astype(BF16))

    def key_start(n):
        return pl.multiple_of((n_tiles - 1 - n) * tk, tk)

    def scores(n, e, slot, diag):
        kt = k_ref[0, e, :, pl.ds(key_start(n), tk)]
        z = _dot(qa_ref[e], kt)
        sp = _softplus2(z)
        if diag:
            col = lax.broadcasted_iota(jnp.int32, (tq, tk), 1) + (tq - (n + 1) * tk)
            valid = col < lax.broadcasted_iota(jnp.int32, (tq, tk), 0)
            sp = jnp.where(valid, sp, 0.0)
            z = jnp.where(valid, z, MASKED_LOGIT)
        z_ref[e, slot] = z
        sp_ref[e, slot] = sp.astype(BF16)

    def weights(e, slot):
        m = _dot(sp_ref[e, slot], u_ref[...])
        a_ref[e, slot] = jnp.exp2(z_ref[e, slot] - m).astype(BF16)
        m0_ref[e, slot] = m[:, 0:1]

    def accumulate(n, e, slot):
        vt = v_ref[0, pl.ds(key_start(n), tk), :]
        pv = _dot(a_ref[e, slot], vt)
        r = r_ref[e]
        acc_ref[e] += jnp.exp2(r) * pv
        r_ref[e] = r - m0_ref[e, slot]

    acc_ref[...] = jnp.zeros_like(acc_ref)
    r_ref[...] = jnp.zeros_like(r_ref)
    for e in pair:
        scores(0, e, 0, True)
    for e in pair:
        scores(1, e, 1, True)
    for e in pair:
        weights(e, 0)

    def body(p, carry):
        s = 2 * p
        for e in pair:
            accumulate(s, e, 0)
            weights(e, 1)
            scores(s + 2, e, 0, False)
        for e in pair:
            accumulate(s + 1, e, 1)
            weights(e, 0)
            scores(s + 3, e, 1, False)
        return carry

    lax.fori_loop(0, qi, body, 0)
    for e in pair:
        accumulate(n_tiles - 2, e, 0)
    for e in pair:
        weights(e, 1)
    for e in pair:
        accumulate(n_tiles - 1, e, 1)
    o_ref[0] = jnp.where(lane < SB_HEAD_DIM, acc_ref[0], acc_ref[1]).astype(o_ref.dtype)


def _sb_prompt(q, k_aug, v, bias_rows, umat):
    nb, t, _ = q.shape
    npair = SB_HEADS // 2
    assert SB_TQ == 2 * SB_TK and 2 * SB_HEAD_DIM == LANES
    return pl.pallas_call(
        _sb_prompt_kernel,
        grid=(nb, npair, t // SB_TQ),
        in_specs=[
            pl.BlockSpec((1, SB_TQ, LANES), lambda b, p, i: (b, i, p)),
            pl.BlockSpec((1, 2, LANES, t), lambda b, p, i: (b * npair + p, 0, 0, 0)),
            pl.BlockSpec((1, t, LANES), lambda b, p, i: (b, 0, p)),
            pl.BlockSpec((1, 2, LANES), lambda b, p, i: (p, 0, 0)),
            pl.BlockSpec((SB_TK, SB_TK), lambda b, p, i: (0, 0)),
        ],
        out_specs=pl.BlockSpec((1, SB_TQ, LANES), lambda b, p, i: (b, i, p)),
        out_shape=jax.ShapeDtypeStruct((nb, t, SB_WIDTH), BF16),
        scratch_shapes=[
            pltpu.VMEM((2, SB_TQ, LANES), BF16),
            pltpu.VMEM((2, 2, SB_TQ, SB_TK), F32),
            pltpu.VMEM((2, 2, SB_TQ, SB_TK), BF16),
            pltpu.VMEM((2, 2, SB_TQ, SB_TK), BF16),
            pltpu.VMEM((2, 2, SB_TQ, 1), F32),
            pltpu.VMEM((2, SB_TQ, LANES), F32),
            pltpu.VMEM((2, SB_TQ, 1), F32),
        ],
        compiler_params=_cparams(("parallel", "parallel", "arbitrary")),
        name="sb_prompt",
    )(q, k_aug, v, bias_rows, umat)


def _sb_decode_kernel(pt_ref, q_ref, kn_ref, vn_ref, bias_ref, u_ref, *rest):
    del pt_ref
    npg = DEC_PAGES_PER_STEP
    kp_refs = rest[:npg]
    vp_refs = rest[npg:2 * npg]
    o_ref, acc_ref, r_ref = rest[2 * npg:]
    g = pl.program_id(1)
    nh, tq = SB_HEADS, DEC_TQ

    def side_by_side(refs):
        return refs[0][0] if len(refs) == 1 else jnp.concatenate([ref[0] for ref in refs], axis=1)

    def scores(k_refs, valid):
        z = _dot(q_ref[0], side_by_side(k_refs)) + bias_ref[...]
        sp = _softplus2(z)
        if valid is not None:
            sp = jnp.where(valid, sp, 0.0)
            z = jnp.where(valid, z, MASKED_LOGIT)
        nk = z.shape[1]
        m = _dot(sp.astype(BF16), u_ref[0:nk, 0:nk])
        return z - m, m[:, 0:1]

    def attend(blocks):
        scored = [scores(k_refs, valid) for k_refs, _, valid in blocks]
        r = r_ref[...]
        out = None
        for (e, m0), (_, v_refs, _) in zip(scored, blocks):
            a = jnp.exp2(e + r)
            pv = _dot_nt(a, side_by_side(v_refs))
            out = pv if out is None else out + pv
            r = r - m0
        r_ref[...] = r
        acc_ref[...] += out

    pages = [([kp_refs[i], kp_refs[i + 1]], [vp_refs[i], vp_refs[i + 1]], None)
             for i in reversed(range(0, npg, 2))]

    @pl.when(g == 0)
    def _():
        acc_ref[...] = jnp.zeros_like(acc_ref)
        r_ref[...] = jnp.zeros_like(r_ref)
        key = lax.broadcasted_iota(jnp.int32, (nh * tq, PAGE_SIZE), 1)
        qry = lax.broadcasted_iota(jnp.int32, (nh * tq, PAGE_SIZE), 0) % tq
        attend([([kn_ref], [vn_ref], key < qry)])

    attend(pages)

    @pl.when(g == pl.num_programs(1) - 1)
    def _():
        acc = acc_ref[...]
        rhead = lax.broadcasted_iota(jnp.int32, acc.shape, 0) // tq
        chead = lax.broadcasted_iota(jnp.int32, acc.shape, 1) // SB_HEAD_DIM
        picked = jnp.where(rhead == chead, acc, 0.0)
        folded = picked[:, 0:SB_HEAD_DIM]
        for h in range(1, nh):
            folded = folded + picked[:, h * SB_HEAD_DIM:(h + 1) * SB_HEAD_DIM]
        o_ref[0] = folded.astype(o_ref.dtype)


def _sb_decode(page_ids, q_rows, k_new, v_new, bias_col, umat, cache_k3, cache_v3):
    nb, n_pages = page_ids.shape
    npg = DEC_PAGES_PER_STEP
    ngroups = n_pages // npg
    nrow = SB_HEADS * DEC_TQ

    def page_spec(i):
        return pl.BlockSpec((1, SB_WIDTH, PAGE_SIZE),
                            lambda b, g, pt: (pt[b, (ngroups - 1 - g) * npg + i], 0, 0))

    per_b = lambda b, g, pt: (b, 0, 0)
    const = lambda b, g, pt: (0, 0)
    grid_spec = pltpu.PrefetchScalarGridSpec(
        num_scalar_prefetch=1,
        grid=(nb, ngroups),
        in_specs=[
            pl.BlockSpec((1, nrow, SB_WIDTH), per_b),
            pl.BlockSpec((1, SB_WIDTH, PAGE_SIZE), per_b),
            pl.BlockSpec((1, SB_WIDTH, PAGE_SIZE), per_b),
            pl.BlockSpec((nrow, 1), const),
            pl.BlockSpec((2 * PAGE_SIZE, 2 * PAGE_SIZE), const),
        ] + [page_spec(i) for i in range(npg)] * 2,
        out_specs=pl.BlockSpec((1, nrow, SB_HEAD_DIM), per_b),
        scratch_shapes=[
            pltpu.VMEM((nrow, SB_WIDTH), F32),
            pltpu.VMEM((nrow, 1), F32),
        ],
    )
    return pl.pallas_call(
        _sb_decode_kernel,
        grid_spec=grid_spec,
        out_shape=jax.ShapeDtypeStruct((nb, nrow, SB_HEAD_DIM), BF16),
        compiler_params=_cparams(("parallel", "arbitrary")),
        name="sb_decode",
    )(page_ids, q_rows, k_new, v_new, bias_col, umat, *([cache_k3] * npg), *([cache_v3] * npg))


def _convb_kernel(b_ref, c_ref, x_ref, cp_ref, xp_ref, buf_ref, w_ref, o_ref, tail_ref, u_ref):
    i = pl.program_id(1)
    tt = b_ref.shape[1]
    u = c_ref[0] * x_ref[0]
    prev = jnp.where(i == 0, buf_ref[0], cp_ref[0] * xp_ref[0])
    u_ref[0:SUBLANES, :] = prev
    u_ref[SUBLANES:, :] = u
    base = SUBLANES - (SC_TAPS - 1)
    y = w_ref[0:1, :] * u_ref[base:base + tt, :]
    for j in range(1, SC_TAPS):
        y = y + w_ref[j:j + 1, :] * u_ref[base + j:base + j + tt, :]
    o_ref[0] = (b_ref[0] * y).astype(o_ref.dtype)
    tail_ref[0] = u[tt - SUBLANES:, :]


def _convb(rest3, buf8, w8, tt):
    nb, t, _ = rest3.shape
    r8 = tt // SUBLANES
    c0 = REST_BCX // COL_TILE
    cur = lambda c: pl.BlockSpec((1, tt, COL_TILE), lambda b, i: (b, i, c))
    prev = lambda c: pl.BlockSpec((1, SUBLANES, COL_TILE), lambda b, i: (b, jnp.maximum(i * r8 - 1, 0), c))
    return pl.pallas_call(
        _convb_kernel,
        grid=(nb, t // tt),
        in_specs=[cur(c0), cur(c0 + 1), cur(c0 + 2), prev(c0 + 1), prev(c0 + 2),
                  pl.BlockSpec((1, SUBLANES, COL_TILE), lambda b, i: (b, 0, 0)),
                  pl.BlockSpec((SUBLANES, COL_TILE), lambda b, i: (0, 0))],
        out_specs=[pl.BlockSpec((1, tt, COL_TILE), lambda b, i: (b, i, 0)),
                   pl.BlockSpec((1, SUBLANES, COL_TILE), lambda b, i: (b, 0, 0))],
        out_shape=[jax.ShapeDtypeStruct((nb, t, COL_TILE), BF16),
                   jax.ShapeDtypeStruct((nb, SUBLANES, COL_TILE), F32)],
        scratch_shapes=[pltpu.VMEM((tt + SUBLANES, COL_TILE), F32)],
        compiler_params=_cparams(("parallel", "arbitrary")),
        name="convb",
    )(rest3, rest3, rest3, rest3, rest3, buf8, w8)


def _split_bf16(x):
    hi = x.astype(BF16)
    return hi, (x - hi.astype(F32)).astype(BF16)


def _dot3(a, b, nt=False):
    ah, al = _split_bf16(a)
    bh, bl = _split_bf16(b)
    d = _dot_nt if nt else _dot
    return d(ah, bh) + (d(al, bh) + d(ah, bl))


def _dotb(a, b):
    return _dot(a.astype(BF16), b.astype(BF16))


def _gdn_kernel(x_ref, z_ref, sm_ref, xp_ref, buf_ref, s0_ref, w_ref, nega_ref, dtb_ref, og_ref,
                o_ref, s_out_ref, s_ref, xb_ref, *, n_valid, chunk):
    i = pl.program_id(1)
    rows = x_ref.shape[1]
    c = chunk
    nh = GDN_HEADS
    hd = GDN_HEAD_DIM

    @pl.when(i == 0)
    def _():
        s_ref[...] = s0_ref[0]

    xb_ref[0:SUBLANES, :] = jnp.where(i == 0, buf_ref[0], xp_ref[0])
    xb_ref[SUBLANES:, :] = x_ref[0]
    base = SUBLANES - (GDN_TAPS - 1)
    y = w_ref[0:1, :] * xb_ref[base:base + rows, :]
    for j in range(1, GDN_TAPS):
        y = y + w_ref[j:j + 1, :] * xb_ref[base + j:base + j + rows, :]
    act = y * jax.nn.sigmoid(y)

    sm = sm_ref[0]
    beta_all = jax.nn.sigmoid(sm)
    pre = sm + dtb_ref[...]
    g_all = nega_ref[...] * (jnp.maximum(pre, 0.0) + jnp.log1p(jnp.exp(-jnp.abs(pre))))
    if n_valid < rows:
        live = lax.broadcasted_iota(jnp.int32, (rows, LANES), 0) < n_valid
        beta_all = jnp.where(live, beta_all, 0.0)
        g_all = jnp.where(live, g_all, 0.0)

    rr = lax.broadcasted_iota(jnp.int32, (rows, rows), 0)
    cc = lax.broadcasted_iota(jnp.int32, (rows, rows), 1)
    gcum = _dot3(((rr // c == cc // c) & (cc <= rr)).astype(F32), g_all)

    row = lax.broadcasted_iota(jnp.int32, (c, c), 0)
    col = lax.broadcasted_iota(jnp.int32, (c, c), 1)
    incl = col <= row
    strict = col < row
    wide = nh * c
    eye_w = (lax.broadcasted_iota(jnp.int32, (c, wide), 0)
             == lax.broadcasted_iota(jnp.int32, (c, wide), 1) % c).astype(F32)
    same_block = (lax.broadcasted_iota(jnp.int32, (wide, wide), 0) // c
                  == lax.broadcasted_iota(jnp.int32, (wide, wide), 1) // c)

    def times_blockdiag(x, p):
        xh, xl = _split_bf16(x)
        ph, pl_ = _split_bf16(p)
        zero = jnp.zeros((), BF16)
        bdh = jnp.where(same_block, jnp.tile(ph, (nh, 1)), zero)
        bdl = jnp.where(same_block, jnp.tile(pl_, (nh, 1)), zero)
        return _dot(xh, bdh) + (_dot(xl, bdh) + _dot(xh, bdl))

    nchunks = rows // c
    chunks = []
    for ci in range(nchunks):
        r0 = ci * c
        heads = []
        gcum_t = gcum[r0:r0 + c, :].T
        for h in range(nh):
            qh = act[r0:r0 + c, h * hd:(h + 1) * hd]
            kh = act[r0:r0 + c, GDN_WIDTH + h * hd:GDN_WIDTH + (h + 1) * hd]
            vh = act[r0:r0 + c, 2 * GDN_WIDTH + h * hd:2 * GDN_WIDTH + (h + 1) * hd]
            qh = qh * lax.rsqrt(jnp.sum(qh * qh, axis=-1, keepdims=True) + EPS) * (hd ** -0.5)
            kh = kh * lax.rsqrt(jnp.sum(kh * kh, axis=-1, keepdims=True) + EPS)
            b = beta_all[r0:r0 + c, h:h + 1]
            gc = gcum[r0:r0 + c, nh + h:nh + h + 1]
            d = gc - gcum_t[nh + h:nh + h + 1, :]
            decay = jnp.where(incl, jnp.exp(jnp.where(incl, d, 0.0)), 0.0)
            a = jnp.where(strict, b * decay * _dot3(kh, kh, nt=True), 0.0)
            heads.append((qh, kh, vh, b, gc, decay, a))
        chunks.append(heads)

    ps = [jnp.concatenate([hh[6] for hh in heads], axis=1) for heads in chunks]
    tinvs = [eye_w - p for p in ps]
    for _ in range(int(math.log2(c)) - 1):
        ps = [times_blockdiag(p, p) for p in ps]
        tinvs = [t + times_blockdiag(t, p) for t, p in zip(tinvs, ps)]

    def prepare(ci):
        per_head = []
        for h, (qh, kh, vh, b, gc, decay, _) in enumerate(chunks[ci]):
            th = tinvs[ci][:, h * c:(h + 1) * c]
            eg = jnp.exp(gc)
            g_last = gc[c - 1:c, :]
            u = _dotb(th, b * vh)
            wk = _dotb(th, (b * eg) * kh).astype(BF16)
            qk = jnp.where(incl, _dot_nt(qh.astype(BF16), kh.astype(BF16)) * decay, 0.0).astype(BF16)
            per_head.append((u, wk, qk, (eg * qh).astype(BF16),
                             (kh * jnp.exp(g_last - gc)).astype(BF16), jnp.exp(g_last)))
        return per_head

    state = [s_ref[h] for h in range(nh)]
    per_head = prepare(0)
    for ci in range(nchunks):
        r0 = ci * c
        for h, (u, wk, qk, egq, kd, dec_last) in enumerate(per_head):
            s = state[h]
            sb = s.astype(BF16)
            w = u - _dot(wk, sb)
            wb = w.astype(BF16)
            o = _dot(egq, sb) + _dot(qk, wb)
            state[h] = dec_last * s + _dot_tn(kd, wb)
            on = o * lax.rsqrt(jnp.mean(o * o, axis=-1, keepdims=True) + EPS) * og_ref[...]
            zh = z_ref[0, r0:r0 + c, h * hd:(h + 1) * hd]
            o_ref[0, r0:r0 + c, h * hd:(h + 1) * hd] = (on * (zh * jax.nn.sigmoid(zh))).astype(o_ref.dtype)
        if ci + 1 < nchunks:
            per_head = prepare(ci + 1)
    for h in range(nh):
        s_ref[h] = state[h]

    @pl.when(i == pl.num_programs(1) - 1)
    def _():
        s_out_ref[0] = s_ref[...]


def _gdn(rest3, small3, buf8, s0, w8, nega_row, dtb_row, og_row, c, n_valid, chunk):
    nb, t, _ = rest3.shape
    r8 = c // SUBLANES
    qkv_w = 3 * GDN_WIDTH
    kern = functools.partial(_gdn_kernel, n_valid=n_valid, chunk=chunk)
    const2 = lambda b, i: (0, 0)
    return pl.pallas_call(
        kern,
        grid=(nb, t // c),
        in_specs=[
            pl.BlockSpec((1, c, qkv_w), lambda b, i: (b, i, REST_QKV // qkv_w)),
            pl.BlockSpec((1, c, GDN_WIDTH), lambda b, i: (b, i, REST_Z // GDN_WIDTH)),
            pl.BlockSpec((1, c, LANES), lambda b, i: (b, i, 0)),
            pl.BlockSpec((1, SUBLANES, qkv_w), lambda b, i: (b, jnp.maximum(i * r8 - 1, 0), REST_QKV // qkv_w)),
            pl.BlockSpec((1, SUBLANES, qkv_w), lambda b, i: (b, 0, 0)),
            pl.BlockSpec((1, GDN_HEADS, GDN_HEAD_DIM, GDN_HEAD_DIM), lambda b, i: (b, 0, 0, 0)),
            pl.BlockSpec((SUBLANES, qkv_w), const2),
            pl.BlockSpec((1, LANES), const2),
            pl.BlockSpec((1, LANES), const2),
            pl.BlockSpec((1, GDN_HEAD_DIM), const2),
        ],
        out_specs=[
            pl.BlockSpec((1, c, GDN_WIDTH), lambda b, i: (b, i, 0)),
            pl.BlockSpec((1, GDN_HEADS, GDN_HEAD_DIM, GDN_HEAD_DIM), lambda b, i: (b, 0, 0, 0)),
        ],
        out_shape=[
            jax.ShapeDtypeStruct((nb, t, GDN_WIDTH), BF16),
            jax.ShapeDtypeStruct((nb, GDN_HEADS, GDN_HEAD_DIM, GDN_HEAD_DIM), F32),
        ],
        scratch_shapes=[
            pltpu.VMEM((GDN_HEADS, GDN_HEAD_DIM, GDN_HEAD_DIM), F32),
            pltpu.VMEM((c + SUBLANES, qkv_w), F32),
        ],
        compiler_params=_cparams(("parallel", "arbitrary")),
        name="gdn",
    )(rest3, rest3, small3, rest3, buf8, s0, w8, nega_row, dtb_row, og_row)


def _merge_kernel(x_ref, a_ref, b_ref, c_ref, gl_ref, wpa_ref, wpb_ref, wpc_ref, wo_ref, o_ref):
    d = x_ref.shape[1]
    gl = gl_ref[...]
    mix = jax.nn.sigmoid(gl[:, 0:d]) * _dot(a_ref[...], wpa_ref[...])
    mix = mix + jax.nn.sigmoid(gl[:, d:2 * d]) * _dot(b_ref[...], wpb_ref[...])
    mix = mix + jax.nn.sigmoid(gl[:, 2 * d:3 * d]) * _dot(c_ref[...], wpc_ref[...])
    o_ref[...] = x_ref[...] + _dot(mix.astype(BF16), wo_ref[...])


def _merge(x2d, out_a, out_b, out_c, rest, wpa, wpb, wpc, wo, tm):
    n, d = x2d.shape
    row = lambda i: (i, 0)
    const = lambda i: (0, 0)
    return pl.pallas_call(
        _merge_kernel,
        grid=(n // tm,),
        in_specs=[
            pl.BlockSpec((tm, d), row),
            pl.BlockSpec((tm, COL_TILE), row),
            pl.BlockSpec((tm, COL_TILE), row),
            pl.BlockSpec((tm, COL_TILE), row),
            pl.BlockSpec((tm, N_BRANCH * d), row),
            pl.BlockSpec((COL_TILE, d), const),
            pl.BlockSpec((COL_TILE, d), const),
            pl.BlockSpec((COL_TILE, d), const),
            pl.BlockSpec((d, d), const),
        ],
        out_specs=pl.BlockSpec((tm, d), row),
        out_shape=jax.ShapeDtypeStruct((n, d), F32),
        compiler_params=_cparams(("parallel",)),
        name="merge",
    )(x2d, out_a, out_b, out_c, rest, wpa, wpb, wpc, wo)


def _mlp_kernel(x_ref, g_ref, wu_ref, wd_ref, o_ref, h_ref, acc_ref):
    j = pl.program_id(1)

    @pl.when(j == 0)
    def _():
        x = x_ref[...]
        ms = jnp.mean(x * x, axis=-1, keepdims=True)
        h_ref[...] = (x * lax.rsqrt(ms + EPS) * g_ref[...]).astype(BF16)
        acc_ref[...] = x

    u = jnp.maximum(_dot(h_ref[...], wu_ref[...]), 0.0)
    acc_ref[...] += _dot((u * u).astype(BF16), wd_ref[...])

    @pl.when(j == pl.num_programs(1) - 1)
    def _():
        o_ref[...] = acc_ref[...]


def _mlp(x2d, g, w_up, w_down, tm, tf):
    n, d = x2d.shape
    dff = w_up.shape[1]
    return pl.pallas_call(
        _mlp_kernel,
        grid=(n // tm, dff // tf),
        in_specs=[
            pl.BlockSpec((tm, d), lambda i, j: (i, 0)),
            pl.BlockSpec((1, d), lambda i, j: (0, 0)),
            pl.BlockSpec((d, tf), lambda i, j: (0, j)),
            pl.BlockSpec((tf, d), lambda i, j: (j, 0)),
        ],
        out_specs=pl.BlockSpec((tm, d), lambda i, j: (i, 0)),
        out_shape=jax.ShapeDtypeStruct((n, d), F32),
        scratch_shapes=[pltpu.VMEM((tm, d), BF16), pltpu.VMEM((tm, d), F32)],
        compiler_params=_cparams(("parallel", "arbitrary")),
        name="mlp",
    )(x2d, g, w_up, w_down)


def _pad_rows_front(a, rows):
    pad = rows - a.shape[1]
    return jnp.pad(a, ((0, 0), (pad, 0), (0, 0)))


def _split3_bf16(v):
    hi = v.astype(BF16)
    mid = (v - hi.astype(F32)).astype(BF16)
    lo = (v - hi.astype(F32) - mid.astype(F32)).astype(BF16)
    return hi, mid, lo


def _layer_weights(l, norm1_g, w_in, qn_g, kn_g, sb_bias, w_conv_b, w_conv_c, a_log, dt_bias, onorm_g,
                   w_pa, w_pb, w_pc, w_o, norm2_g, w_up, w_down):
    d = w_in.shape[1]
    main_w = 3 * SB_WIDTH + 3 * SC_WIDTH + 4 * GDN_WIDTH
    small_w = 2 * GDN_HEADS
    wl = w_in[l]
    w_main = jnp.concatenate([wl[:, :3 * SB_WIDTH], wl[:, main_w + small_w:], wl[:, 3 * SB_WIDTH:main_w]],
                             axis=1).astype(BF16)
    w_small = jnp.pad(wl[:, main_w:main_w + small_w], ((0, 0), (0, LANES - small_w))).astype(BF16)
    head_of = jnp.arange(SB_WIDTH) // SB_HEAD_DIM
    pmat = (head_of[:, None] == head_of[None, :]).astype(F32) / SB_HEAD_DIM
    bias2 = sb_bias[l].astype(F32) * LOG2E
    nega = -jnp.exp(a_log[l].astype(F32))
    zeros4 = jnp.zeros((GDN_HEADS,), F32)
    pad_lanes = lambda v: jnp.pad(v, (0, LANES - v.shape[0]))[None, :]
    return dict(
        g1=norm1_g[l][None, :], w_main=w_main, w_small=w_small,
        qg=jnp.tile(qn_g[l], SB_HEADS)[None, :], kg=jnp.tile(kn_g[l], SB_HEADS)[None, :],
        pmat=pmat.astype(BF16), bias2=bias2,
        wcb=jnp.pad(w_conv_b[l], ((0, SUBLANES - SC_TAPS), (0, 0))),
        wcc=jnp.pad(w_conv_c[l], ((0, SUBLANES - GDN_TAPS), (0, 0))),
        nega_row=pad_lanes(jnp.concatenate([zeros4, nega])),
        dtb_row=pad_lanes(jnp.concatenate([zeros4, dt_bias[l].astype(F32)])),
        og_row=onorm_g[l][None, :],
        wpa=w_pa[l].astype(BF16), wpb=w_pb[l].astype(BF16), wpc=w_pc[l].astype(BF16),
        wo=w_o[l].astype(BF16), g2=norm2_g[l][None, :],
        w_up=w_up[l].astype(BF16), w_down=w_down[l].astype(BF16),
    )


def _prompt_attention(q, k, v, bias2):
    nb, t, _ = q.shape
    npair = SB_HEADS // 2
    hi, mid, lo = _split3_bf16(bias2)
    parts = jnp.stack([hi, mid, lo], axis=-1).astype(F32).reshape(npair, 2, 3)
    parts = jnp.pad(parts, ((0, 0), (0, 0), (0, SB_HEAD_DIM - 3)))
    zeros = jnp.zeros_like(parts[:, 0])
    bias_rows = jnp.stack([jnp.concatenate([zeros, parts[:, 0]], axis=-1),
                           jnp.concatenate([parts[:, 1], zeros], axis=-1)], axis=1)
    k_t = k.astype(BF16).reshape(nb, t, npair, 2, SB_HEAD_DIM).transpose(0, 2, 3, 4, 1)
    ones_rows = jnp.concatenate([jnp.ones((nb, npair, 3, t), BF16),
                                 jnp.zeros((nb, npair, SB_HEAD_DIM - 3, t), BF16)], axis=2)
    k_aug = jnp.stack([jnp.concatenate([k_t[:, :, 0], ones_rows], axis=2),
                       jnp.concatenate([ones_rows, k_t[:, :, 1]], axis=2)], axis=2)
    idx = jnp.arange(SB_TK)
    umat = (idx[:, None] >= idx[None, :]).astype(BF16)
    o = _sb_prompt(q, k_aug.reshape(nb * npair, 2, LANES, t), v.astype(BF16), bias_rows, umat)
    return o.reshape(nb * t, SB_WIDTH)


def _decode_attention(q, k, v, bias2, page_ids, cache_k2, cache_v2):
    nb, tq, _ = q.shape
    nrow = SB_HEADS * tq
    q_rep = jnp.tile(q.astype(F32), (1, SB_HEADS, 1)).reshape(nb, SB_HEADS, tq, SB_WIDTH)
    head_cols = (jnp.arange(SB_WIDTH)[None, :] // SB_HEAD_DIM) == jnp.arange(SB_HEADS)[:, None]
    q_rows = jnp.where(head_cols[None, :, None, :], q_rep, 0.0).reshape(nb, nrow, SB_WIDTH)
    bias_col = jnp.repeat(bias2, tq)[:, None]
    idx = jnp.arange(2 * PAGE_SIZE)
    umat = (idx[:, None] >= idx[None, :]).astype(BF16)
    as_page = lambda a: jnp.pad(a.transpose(0, 2, 1), ((0, 0), (0, 0), (0, PAGE_SIZE - tq)))
    o = _sb_decode(page_ids, q_rows, as_page(k), as_page(v), bias_col, umat, cache_k2, cache_v2)
    o = o.reshape(nb, SB_HEADS, tq, SB_HEAD_DIM).transpose(0, 2, 1, 3)
    return o.reshape(nb * tq, SB_WIDTH)


def _trunk_layer(x3, w, attend, buf_b, buf_c, s0, chunk, n_valid):
    nb, t, d = x3.shape
    n = nb * t
    tm = min(ROW_TILE, n)
    q, k, v, rest, small = _proj(x3.reshape(n, d), w["g1"], w["w_main"], w["w_small"],
                                 w["qg"], w["kg"], w["pmat"], min(PROJ_ROW_TILE, n))
    out_a = attend(q.reshape(nb, t, SB_WIDTH), k.reshape(nb, t, SB_WIDTH), v.reshape(nb, t, SB_WIDTH))
    rest3 = rest.reshape(nb, t, rest.shape[1])
    out_b, tail_b = _convb(rest3, _pad_rows_front(buf_b, SUBLANES), w["wcb"], min(512, t))
    cs = min(GDN_CHUNK, -(-t // GDN_MIN_CHUNK) * GDN_MIN_CHUNK)
    tg = -(-t // cs) * cs
    pad_t = lambda a: jnp.pad(a, ((0, 0), (0, tg - t), (0, 0)))
    out_c, s_new = _gdn(pad_t(rest3), pad_t(small.reshape(nb, t, LANES)), _pad_rows_front(buf_c, SUBLANES), s0,
                        w["wcc"], w["nega_row"], w["dtb_row"], w["og_row"], min(chunk, tg), n_valid, cs)
    out_c = out_c[:, :t]
    x1 = _merge(x3.reshape(n, d), out_a, out_b.reshape(n, SC_WIDTH), out_c.reshape(n, GDN_WIDTH), rest,
                w["wpa"], w["wpb"], w["wpc"], w["wo"], tm)
    x2 = _mlp(x1, w["g2"], w["w_up"], w["w_down"], min(MLP_ROW_TILE, n), MLP_FF_TILE)
    nv = n_valid - (t - SUBLANES)
    new_b = tail_b[:, nv - (SC_TAPS - 1):nv]
    new_c = rest3[:, n_valid - (GDN_TAPS - 1):n_valid, REST_QKV:REST_QKV + 3 * GDN_WIDTH]
    return x2.reshape(nb, t, d), k, v, new_b, new_c, s_new


def kernel(x_prompt, x_sample, cache_k, cache_v, page_table, state_conv_b, state_conv_c, state_gdn,
           norm1_g, w_in, qn_g, kn_g, sb_bias, w_conv_b, w_conv_c, a_log, dt_bias, onorm_g,
           w_pa, w_pb, w_pc, w_o, norm2_g, w_up, w_down):
    depth = w_in.shape[0]
    nbp, tp, d = x_prompt.shape
    nbs, ts, _ = x_sample.shape
    n_phys = cache_k.shape[1]
    assert ts <= DEC_TQ and ts >= GDN_TAPS - 1 and tp % SB_TQ == 0
    assert page_table.shape[1] % DEC_PAGES_PER_STEP == 0 and cache_k.shape[2] == PAGE_SIZE

    xp = x_prompt
    xs = jnp.pad(x_sample, ((0, 0), (0, DEC_TQ - ts), (0, 0)))
    cache_k2 = cache_k.transpose(0, 1, 3, 4, 2).reshape(depth * n_phys, SB_WIDTH, PAGE_SIZE)
    cache_v2 = cache_v.transpose(0, 1, 3, 4, 2).reshape(depth * n_phys, SB_WIDTH, PAGE_SIZE)

    outs_p, outs_s = [], []
    for l in range(depth):
        w = _layer_weights(l, norm1_g, w_in, qn_g, kn_g, sb_bias, w_conv_b, w_conv_c, a_log, dt_bias,
                           onorm_g, w_pa, w_pb, w_pc, w_o, norm2_g, w_up, w_down)
        zb = jnp.zeros((nbp, SC_TAPS - 1, SC_WIDTH), F32)
        zc = jnp.zeros((nbp, GDN_TAPS - 1, 3 * GDN_WIDTH), F32)
        zs = jnp.zeros((nbp, GDN_HEADS, GDN_HEAD_DIM, GDN_HEAD_DIM), F32)
        attend_p = functools.partial(_prompt_attention, bias2=w["bias2"])
        xp, k_p, v_p, cb_p, cc_p, s_p = _trunk_layer(xp, w, attend_p, zb, zc, zs, GDN_ROWS_PER_STEP, tp)
        page_ids = page_table.astype(jnp.int32) + l * n_phys
        attend_s = functools.partial(_decode_attention, bias2=w["bias2"], page_ids=page_ids,
                                     cache_k2=cache_k2, cache_v2=cache_v2)
        xs, k_s, v_s, cb_s, cc_s, s_s = _trunk_layer(xs, w, attend_s, state_conv_b[l], state_conv_c[l],
                                                     state_gdn[l], GDN_ROWS_PER_STEP, ts)
        outs_p.append((k_p.reshape(nbp, tp, SB_HEADS, SB_HEAD_DIM), v_p.reshape(nbp, tp, SB_HEADS, SB_HEAD_DIM),
                       cb_p, cc_p, s_p))
        outs_s.append((k_s.reshape(nbs, DEC_TQ, SB_HEADS, SB_HEAD_DIM)[:, :ts],
                       v_s.reshape(nbs, DEC_TQ, SB_HEADS, SB_HEAD_DIM)[:, :ts], cb_s, cc_s, s_s))

    stack = lambda outs, i: jnp.stack([o[i] for o in outs])
    return (xp, xs[:, :ts],
            stack(outs_p, 0), stack(outs_p, 1), stack(outs_p, 2), stack(outs_p, 3), stack(outs_p, 4),
            stack(outs_s, 0), stack(outs_s, 1), stack(outs_s, 2), stack(outs_s, 3), stack(outs_s, 4))
```
